```python
import jax
import jax.numpy as jnp
from jax import lax
import numpy as np

D_MODEL = 2048
BATCH = 2
SEQ = 4096
DEPTH = 4
DEC_BATCH = 8
DEC_SEQ = 8
PAST_LEN = 16384
PAGE_SIZE = 128

N_MIXERS = 2
N_ATTN_LAYERS = (DEPTH + 1) // 2
N_RG_LAYERS = DEPTH // 2
HEAD_DIM = 128
HEADS_PER_GROUP = 8
DILATED_GROUPS = ((128, 1), (512, 4), (2048, 16))
N_GROUPS = len(DILATED_GROUPS)
QKV_WIDTH = N_GROUPS * 3 * HEADS_PER_GROUP * HEAD_DIM
ATTN_OUT_WIDTH = HEADS_PER_GROUP * HEAD_DIM
ROPE_THETA = 10000.0
D_RNN = D_MODEL
RG_BLOCKS = 8
RG_BLOCK = D_RNN // RG_BLOCKS
CONV_W = 4
LRU_C = 8.0
N_EXPERTS = 32
TOP_K = 4
D_EXPERT = D_MODEL
SWIGLU_LIMIT = 7.0
SWIGLU_ALPHA = 1.702
EXPERT_BLOCK = 128
NORM_EPS = 1e-6

kernel_name = 'hybrid_dilated_swa_rglru_moe_adaln_step'


def rmsnorm(x, g):
    xf = x.astype(jnp.float32)
    xf = xf * lax.rsqrt(jnp.mean(xf * xf, axis=-1, keepdims=True) + NORM_EPS)
    return xf.astype(x.dtype) * g


def rope_tables(positions):
    inv_freq = ROPE_THETA ** (-jnp.arange(0, HEAD_DIM, 2, dtype=jnp.float32) / HEAD_DIM)
    ang = positions.astype(jnp.float32)[:, None] * inv_freq[None, :]
    return jnp.cos(ang), jnp.sin(ang)


def apply_rope(t, cos, sin):
    t1, t2 = jnp.split(t.astype(jnp.float32), 2, axis=-1)
    c = cos[None, :, None, :]
    s = sin[None, :, None, :]
    return jnp.concatenate([t1 * c - t2 * s, t2 * c + t1 * s], axis=-1).astype(t.dtype)


def dilated_band_attention(q, k, v, window, dil):
    B, S, H, Dh = q.shape
    nk = window // dil
    span = nk * dil
    s_pad = -(-S // span) * span
    n_blk = s_pad // span

    def to_blocks(t):
        t = jnp.pad(t, ((0, 0), (0, s_pad - S), (0, 0), (0, 0)))
        t = t.reshape(B, n_blk, nk, dil, H, Dh)
        return t.transpose(0, 3, 1, 2, 4, 5)

    def with_prev(t):
        prev = jnp.pad(t, ((0, 0), (0, 0), (1, 0), (0, 0), (0, 0), (0, 0)))[:, :, :-1]
        return jnp.concatenate([prev, t], axis=3)

    def from_blocks(t):
        t = jnp.moveaxis(t, 1, 3)
        return t.reshape((B, s_pad) + t.shape[4:])[:, :S]

    qb = to_blocks(q)
    kb = with_prev(to_blocks(k))
    vb = with_prev(to_blocks(v))
    s = jnp.einsum('brnihd,brnjhd->brnhij', qb, kb,
                   preferred_element_type=jnp.float32) * (Dh ** -0.5)
    i = jnp.arange(nk)[:, None]
    j = jnp.arange(2 * nk)[None, :]
    band = (j >= i) & (j <= i + nk)
    has_prev = jnp.arange(n_blk)[:, None, None] > 0
    valid = band[None] & (has_prev | (j >= nk)[None])
    s = jnp.where(valid[:, None], s, -jnp.inf)
    m = jnp.max(s, axis=-1, keepdims=True)
    p = jnp.exp(s - m)
    l = jnp.sum(p, axis=-1, keepdims=True)
    o = jnp.einsum('brnhij,brnjhd->brnihd', p / l, vb.astype(jnp.float32))
    lse = jnp.swapaxes((m + jnp.log(l))[..., 0], -1, -2)
    return from_blocks(o), from_blocks(lse)


def dilated_gather_attention(q, k, v, k_buf, v_buf, window, dil):
    B, T, H, Dh = q.shape
    n_past = k_buf.shape[1]
    kk = jnp.concatenate([k_buf, k], axis=1)
    vv = jnp.concatenate([v_buf, v], axis=1)
    idx = n_past + jnp.arange(T)[:, None] - dil * jnp.arange(window // dil + 1)[None, :]
    valid = idx >= 0
    idx = jnp.maximum(idx, 0)
    kg = kk[:, idx]
    vg = vv[:, idx]
    s = jnp.einsum('bthd,btjhd->bthj', q, kg,
                   preferred_element_type=jnp.float32) * (Dh ** -0.5)
    s = jnp.where(valid[:, None, :], s, -jnp.inf)
    m = jnp.max(s, axis=-1, keepdims=True)
    p = jnp.exp(s - m)
    l = jnp.sum(p, axis=-1, keepdims=True)
    o = jnp.einsum('bthj,btjhd->bthd', p / l, vg.astype(jnp.float32))
    return o, (m + jnp.log(l))[..., 0]


def dilated_attention(h, w_qkv, w_o, positions, bufs):
    B, T, _ = h.shape
    qkv = (h @ w_qkv).reshape(B, T, N_GROUPS, 3, HEADS_PER_GROUP, HEAD_DIM)
    cos, sin = rope_tables(positions)
    outs, lses, new_kv = [], [], []
    for g, (window, dil) in enumerate(DILATED_GROUPS):
        q = apply_rope(qkv[:, :, g, 0], cos, sin)
        k = apply_rope(qkv[:, :, g, 1], cos, sin)
        v = qkv[:, :, g, 2]
        if bufs is None:
            o, lse = dilated_band_attention(q, k, v, window, dil)
            keep = min(window, T)
            new_kv += [k[:, T - keep:], v[:, T - keep:]]
        else:
            o, lse = dilated_gather_attention(q, k, v, bufs[g][0], bufs[g][1], window, dil)
            new_kv += [k, v]
        outs.append(o)
        lses.append(lse)
    weights = jax.nn.softmax(jnp.stack(lses, axis=0), axis=0)
    comb = jnp.einsum('gbth,gbthd->bthd', weights, jnp.stack(outs, axis=0))
    out = comb.astype(h.dtype).reshape(B, T, ATTN_OUT_WIDTH) @ w_o
    return out, new_kv


def rglru_mixer(h, w_in, conv_w, conv_b, w_ga, b_ga, w_gx, b_gx, lam, w_out, h0, conv_buf):
    B, T, _ = h.shape
    gate_branch, xr = jnp.split(h @ w_in, 2, axis=-1)
    xp = jnp.concatenate([conv_buf, xr], axis=1)
    xc = conv_b + sum(xp[:, tap:tap + T] * conv_w[tap] for tap in range(CONV_W))
    new_conv_buf = xp[:, T:]
    xb = xc.reshape(B, T, RG_BLOCKS, RG_BLOCK)
    r = jax.nn.sigmoid(jnp.einsum('btnc,ncd->btnd', xb, w_ga).reshape(B, T, D_RNN) + b_ga)
    i = jax.nn.sigmoid(jnp.einsum('btnc,ncd->btnd', xb, w_gx).reshape(B, T, D_RNN) + b_gx)
    log_a = -LRU_C * r.astype(jnp.float32) * jax.nn.softplus(-lam.astype(jnp.float32))
    a = jnp.exp(log_a)
    b = jnp.sqrt(-jnp.expm1(2.0 * log_a)) * (i * xc).astype(jnp.float32)
    b = b.at[:, 0].add(a[:, 0] * h0.astype(jnp.float32))

    def combine(left, right):
        return left[0] * right[0], right[0] * left[1] + right[1]

    _, hs = lax.associative_scan(combine, (a, b), axis=1)
    y = (hs.astype(h.dtype) * jax.nn.gelu(gate_branch)) @ w_out
    return y, hs[:, -1].astype(h.dtype), new_conv_buf


def clamped_swiglu(u):
    glu, lin = jnp.split(u, 2, axis=-1)
    glu = jnp.minimum(glu, SWIGLU_LIMIT)
    lin = jnp.clip(lin, -SWIGLU_LIMIT, SWIGLU_LIMIT)
    return glu * jax.nn.sigmoid(SWIGLU_ALPHA * glu) * (lin + 1.0)


def moe_ffn(h, w_router, b_router, w1, b1, w2, b2):
    shp = h.shape
    xt = h.reshape(-1, shp[-1])
    n_tok = xt.shape[0]
    logits = xt.astype(jnp.float32) @ w_router.astype(jnp.float32) + b_router.astype(jnp.float32)
    top_vals, top_idx = lax.top_k(logits, TOP_K)
    gates = jax.nn.softmax(top_vals, axis=-1).astype(h.dtype)
    n_assign = n_tok * TOP_K
    flat_e = top_idx.reshape(-1)
    order = jnp.argsort(flat_e)
    sorted_e = flat_e[order]
    src_tok = order // TOP_K
    counts = jnp.bincount(flat_e, length=N_EXPERTS)
    padded = (counts + EXPERT_BLOCK - 1) // EXPERT_BLOCK * EXPERT_BLOCK
    pad_end = jnp.cumsum(padded)
    pad_start = pad_end - padded
    start = jnp.cumsum(counts) - counts
    dest = pad_start[sorted_e] + jnp.arange(n_assign) - start[sorted_e]
    n_blocks = -(-(n_assign + N_EXPERTS * (EXPERT_BLOCK - 1)) // EXPERT_BLOCK)
    rows = n_blocks * EXPERT_BLOCK
    x_rows = jnp.zeros((rows, shp[-1]), xt.dtype).at[dest].set(xt[src_tok])
    block_expert = jnp.minimum(
        jnp.searchsorted(pad_end, jnp.arange(n_blocks) * EXPERT_BLOCK, side='right'),
        N_EXPERTS - 1)

    def expert_block(args):
        xb, e = args
        return clamped_swiglu(xb @ w1[e] + b1[e]) @ w2[e] + b2[e]

    y_rows = lax.map(expert_block,
                     (x_rows.reshape(n_blocks, EXPERT_BLOCK, shp[-1]), block_expert))
    y_rows = y_rows.reshape(rows, shp[-1])
    y_assign = jnp.zeros((n_assign, shp[-1]), y_rows.dtype).at[order].set(y_rows[dest])
    y = jnp.einsum('tkd,tk->td', y_assign.reshape(n_tok, TOP_K, shp[-1]), gates)
    return y.reshape(shp).astype(h.dtype)


def run_trunk(x, c, positions, kv_bufs, h_init, conv_init, params):
    (w_ada, b_ada, g_norm1, g_norm2, w_qkv, w_attn_o, w_rg_in, conv_w, conv_b,
     w_gate_a, b_gate_a, w_gate_x, b_gate_x, lru_lambda, w_rg_out,
     w_router, b_router, w_exp1, b_exp1, w_exp2, b_exp2, g_final) = params
    B = x.shape[0]
    kv_new = [[] for _ in range(2 * N_GROUPS)]
    h_new, conv_new = [], []
    cs = jax.nn.silu(c)
    for layer in range(DEPTH):
        mod = (cs @ w_ada[layer] + b_ada[layer]).reshape(B, 6, D_MODEL)[:, :, None, :]
        shift1, scale1, gate1, shift2, scale2, gate2 = [mod[:, n] for n in range(6)]
        hn = rmsnorm(x, g_norm1[layer]) * (1.0 + scale1) + shift1
        j = layer // N_MIXERS
        if layer % N_MIXERS == 0:
            bufs = None if kv_bufs is None else [
                (kv_bufs[2 * g][j], kv_bufs[2 * g + 1][j]) for g in range(N_GROUPS)]
            out, kv = dilated_attention(hn, w_qkv[j], w_attn_o[j], positions, bufs)
            for slot in range(2 * N_GROUPS):
                kv_new[slot].append(kv[slot])
        else:
            h0 = jnp.zeros((B, D_RNN), x.dtype) if h_init is None else h_init[j]
            cb0 = jnp.zeros((B, CONV_W - 1, D_RNN), x.dtype) if conv_init is None else conv_init[j]
            out, h_last, cb = rglru_mixer(hn, w_rg_in[j], conv_w[j], conv_b[j],
                                          w_gate_a[j], b_gate_a[j], w_gate_x[j], b_gate_x[j],
                                          lru_lambda[j], w_rg_out[j], h0, cb0)
            h_new.append(h_last)
            conv_new.append(cb)
        x = x + gate1 * out
        hn = rmsnorm(x, g_norm2[layer]) * (1.0 + scale2) + shift2
        x = x + gate2 * moe_ffn(hn, w_router[layer], b_router[layer], w_exp1[layer],
                                b_exp1[layer], w_exp2[layer], b_exp2[layer])
    y = rmsnorm(x, g_final)
    return y, [jnp.stack(a, axis=0) for a in kv_new], jnp.stack(h_new, axis=0), jnp.stack(conv_new, axis=0)


def setup_inputs(seed: int = 0) -> dict:
    key = jax.random.key(seed)
    ks = iter(jax.random.split(key, 48))
    f32 = jnp.float32

    def nrm(shape, scale):
        return jax.random.normal(next(ks), shape, f32) * scale

    def gain(shape):
        return 1.0 + 0.1 * jax.random.normal(next(ks), shape, f32)

    buf = [min(w, PAST_LEN) for w, _ in DILATED_GROUPS]
    a_base = jax.random.uniform(next(ks), (N_RG_LAYERS, D_RNN), f32, 0.9, 0.999)
    return {
        'x_prompt': nrm((BATCH, SEQ, D_MODEL), 1.0),
        'x_sample': nrm((DEC_BATCH, DEC_SEQ, D_MODEL), 1.0),
        'c_prompt': nrm((BATCH, D_MODEL), 1.0),
        'c_sample': nrm((DEC_BATCH, D_MODEL), 1.0),
        'cache_k_w128': nrm((N_ATTN_LAYERS, DEC_BATCH, buf[0], HEADS_PER_GROUP, HEAD_DIM), 1.0),
        'cache_v_w128': nrm((N_ATTN_LAYERS, DEC_BATCH, buf[0], HEADS_PER_GROUP, HEAD_DIM), 1.0),
        'cache_k_w512': nrm((N_ATTN_LAYERS, DEC_BATCH, buf[1], HEADS_PER_GROUP, HEAD_DIM), 1.0),
        'cache_v_w512': nrm((N_ATTN_LAYERS, DEC_BATCH, buf[1], HEADS_PER_GROUP, HEAD_DIM), 1.0),
        'cache_k_w2048': nrm((N_ATTN_LAYERS, DEC_BATCH, buf[2], HEADS_PER_GROUP, HEAD_DIM), 1.0),
        'cache_v_w2048': nrm((N_ATTN_LAYERS, DEC_BATCH, buf[2], HEADS_PER_GROUP, HEAD_DIM), 1.0),
        'state_rglru_h': nrm((N_RG_LAYERS, DEC_BATCH, D_RNN), 0.5),
        'state_conv': nrm((N_RG_LAYERS, DEC_BATCH, CONV_W - 1, D_RNN), 1.0),
        'w_ada': nrm((DEPTH, D_MODEL, 6 * D_MODEL), 0.5 * D_MODEL ** -0.5),
        'b_ada': nrm((DEPTH, 6 * D_MODEL), 0.02),
        'g_norm1': gain((DEPTH, D_MODEL)),
        'g_norm2': gain((DEPTH, D_MODEL)),
        'w_qkv': nrm((N_ATTN_LAYERS, D_MODEL, QKV_WIDTH), D_MODEL ** -0.5),
        'w_attn_o': nrm((N_ATTN_LAYERS, ATTN_OUT_WIDTH, D_MODEL), ATTN_OUT_WIDTH ** -0.5),
        'w_rg_in': nrm((N_RG_LAYERS, D_MODEL, 2 * D_RNN), D_MODEL ** -0.5),
        'conv_w': nrm((N_RG_LAYERS, CONV_W, D_RNN), CONV_W ** -0.5),
        'conv_b': nrm((N_RG_LAYERS, D_RNN), 0.02),
        'w_gate_a': nrm((N_RG_LAYERS, RG_BLOCKS, RG_BLOCK, RG_BLOCK), RG_BLOCK ** -0.5),
        'b_gate_a': nrm((N_RG_LAYERS, D_RNN), 0.02),
        'w_gate_x': nrm((N_RG_LAYERS, RG_BLOCKS, RG_BLOCK, RG_BLOCK), RG_BLOCK ** -0.5),
        'b_gate_x': nrm((N_RG_LAYERS, D_RNN), 0.02),
        'lru_lambda': jnp.log(a_base) - jnp.log1p(-a_base),
        'w_rg_out': nrm((N_RG_LAYERS, D_RNN, D_MODEL), D_RNN ** -0.5),
        'w_router': nrm((DEPTH, D_MODEL, N_EXPERTS), D_MODEL ** -0.5),
        'b_router': nrm((DEPTH, N_EXPERTS), 0.01),
        'w_exp1': nrm((DEPTH, N_EXPERTS, D_MODEL, 2 * D_EXPERT), D_MODEL ** -0.5),
        'b_exp1': nrm((DEPTH, N_EXPERTS, 2 * D_EXPERT), 0.02),
        'w_exp2': nrm((DEPTH, N_EXPERTS, D_EXPERT, D_MODEL), D_EXPERT ** -0.5),
        'b_exp2': nrm((DEPTH, N_EXPERTS, D_MODEL), 0.02),
        'g_final': gain((D_MODEL,)),
    }


def reference(x_prompt, x_sample, c_prompt, c_sample,
              cache_k_w128, cache_v_w128, cache_k_w512, cache_v_w512,
              cache_k_w2048, cache_v_w2048, state_rglru_h, state_conv,
              w_ada, b_ada, g_norm1, g_norm2, w_qkv, w_attn_o,
              w_rg_in, conv_w, conv_b, w_gate_a, b_gate_a, w_gate_x, b_gate_x,
              lru_lambda, w_rg_out, w_router, b_router,
              w_exp1, b_exp1, w_exp2, b_exp2, g_final):
    params = (w_ada, b_ada, g_norm1, g_norm2, w_qkv, w_attn_o, w_rg_in, conv_w, conv_b,
              w_gate_a, b_gate_a, w_gate_x, b_gate_x, lru_lambda, w_rg_out,
              w_router, b_router, w_exp1, b_exp1, w_exp2, b_exp2, g_final)
    pos_prompt = jnp.arange(x_prompt.shape[1], dtype=jnp.int32)
    y_prompt, kv_p, prompt_h, prompt_conv = run_trunk(
        x_prompt, c_prompt, pos_prompt, None, None, None, params)
    pos_sample = PAST_LEN + jnp.arange(x_sample.shape[1], dtype=jnp.int32)
    kv_bufs = (cache_k_w128, cache_v_w128, cache_k_w512, cache_v_w512, cache_k_w2048, cache_v_w2048)
    y_sample, kv_s, sample_h, sample_conv = run_trunk(
        x_sample, c_sample, pos_sample, kv_bufs, state_rglru_h, state_conv, params)
    prompt_k_w128, prompt_v_w128, prompt_k_w512, prompt_v_w512, prompt_k_w2048, prompt_v_w2048 = kv_p
    sample_k_w128, sample_v_w128, sample_k_w512, sample_v_w512, sample_k_w2048, sample_v_w2048 = kv_s
    return (y_prompt, y_sample,
            prompt_k_w128, prompt_v_w128, prompt_k_w512, prompt_v_w512, prompt_k_w2048, prompt_v_w2048,
            prompt_h, prompt_conv,
            sample_k_w128, sample_v_w128, sample_k_w512, sample_v_w512, sample_k_w2048, sample_v_w2048,
            sample_h, sample_conv)
```

```python
import functools

import numpy as np
import jax
import jax.numpy as jnp
from jax import lax
from jax.experimental import pallas as pl
from jax.experimental.pallas import tpu as pltpu

F32 = jnp.float32
BF16 = jnp.bfloat16
I32 = jnp.int32

DILATED_GROUPS = ((128, 1), (512, 4), (2048, 16))
ROPE_THETA = 10000.0
PAST_LEN = 16384
CONV_W = 4
LRU_C = 8.0
TOP_K = 4
SWIGLU_LIMIT = 7.0
SWIGLU_ALPHA = 1.702
NORM_EPS = 1e-6

V7X_VMEM_BYTES = 64 * 1024 * 1024
VMEM_LIMIT_CAP = 56 * 1024 * 1024
LANES = 128
SUBLANES = 8

MOE_TILE = 1280
MOE_SUB = 256
MOE_TJ = 256


def _params(n_axes, vmem_bytes):
    limit = int(min(VMEM_LIMIT_CAP, max(vmem_bytes * 5 // 4 + (4 << 20), 16 << 20)))
    return pltpu.CompilerParams(dimension_semantics=("arbitrary",) * n_axes,
                                vmem_limit_bytes=limit)


def _pick(n, candidates):
    for c in candidates:
        if n % c == 0:
            return c
    return n


class _Stream:
    def __init__(self, batch, seq, per_row):
        self.batch, self.seq, self.per_row = batch, seq, per_row
        self.rows = batch * seq

    def mod_array(self, vec):
        if self.per_row:
            return jnp.repeat(vec, self.seq, axis=0)
        return vec[:, None, :]

    def mod_spec(self, tm, tn, row_of, col_of):
        if self.per_row:
            return pl.BlockSpec((tm, tn), lambda *g: (row_of(*g), col_of(*g)))
        per_batch = self.seq // tm
        return pl.BlockSpec((None, 1, tn), lambda *g: (row_of(*g) // per_batch, 0, col_of(*g)))

    def row_tile(self, cap):
        if self.per_row:
            return self.rows
        return _pick(self.seq, [c for c in (1024, 512, 256, 128, 64, 32, 16, 8) if c <= cap])


def _ada_body(c_ref, w_ref, b_ref, o_ref):
    c = c_ref[...]
    s = (c * jax.nn.sigmoid(c)).astype(BF16)
    o_ref[0] = jnp.dot(s, w_ref[0].astype(BF16), preferred_element_type=F32) + b_ref[0]


def _ada_mod(c_all, w_ada, b_ada):
    n_layers, d, n6 = w_ada.shape
    mc = c_all.shape[0]
    tn = _pick(n6, (1024, 512, 256, 128))
    vmem = 2 * (mc * d * 4 + d * tn * 4 + tn * 4 + mc * tn * 4) + d * tn * 2
    return pl.pallas_call(
        _ada_body,
        grid=(n_layers, n6 // tn),
        in_specs=[pl.BlockSpec((mc, d), lambda l, j: (0, 0)),
                  pl.BlockSpec((1, d, tn), lambda l, j: (l, 0, j)),
                  pl.BlockSpec((1, 1, tn), lambda l, j: (l, 0, j))],
        out_specs=pl.BlockSpec((1, mc, tn), lambda l, j: (l, 0, j)),
        out_shape=jax.ShapeDtypeStruct((n_layers, mc, n6), F32),
        compiler_params=_params(2, vmem),
        name="ada_mod",
    )(c_all, w_ada, b_ada.reshape(n_layers, 1, n6))


def _rms(x):
    return x * lax.rsqrt(jnp.mean(x * x, axis=-1, keepdims=True) + NORM_EPS)


def _norm_body(x_ref, g_ref, o_ref):
    o_ref[...] = (_rms(x_ref[...]) * g_ref[...]).astype(o_ref.dtype)


def _norm_mod_body(x_ref, g_ref, sc_ref, sh_ref, o_ref):
    hn = (_rms(x_ref[...]) * g_ref[...]) * (1.0 + sc_ref[...]) + sh_ref[...]
    o_ref[...] = hn.astype(o_ref.dtype)


def _final_norm(st, x, g):
    d = x.shape[1]
    tt = st.row_tile(512)
    return pl.pallas_call(
        _norm_body,
        grid=(st.rows // tt,),
        in_specs=[pl.BlockSpec((tt, d), lambda i: (i, 0)),
                  pl.BlockSpec((1, d), lambda i: (0, 0))],
        out_specs=pl.BlockSpec((tt, d), lambda i: (i, 0)),
        out_shape=jax.ShapeDtypeStruct((st.rows, d), F32),
        compiler_params=_params(1, 4 * tt * d * 4),
        name="final_norm",
    )(x, g.reshape(1, d))


def _norm_mod(st, x, g, scale, shift):
    d = x.shape[1]
    tt = st.row_tile(512)
    row_of, col_of = (lambda i: i), (lambda i: 0)
    return pl.pallas_call(
        _norm_mod_body,
        grid=(st.rows // tt,),
        in_specs=[pl.BlockSpec((tt, d), lambda i: (i, 0)),
                  pl.BlockSpec((1, d), lambda i: (0, 0)),
                  st.mod_spec(tt, d, row_of, col_of),
                  st.mod_spec(tt, d, row_of, col_of)],
        out_specs=pl.BlockSpec((tt, d), lambda i: (i, 0)),
        out_shape=jax.ShapeDtypeStruct((st.rows, d), BF16),
        compiler_params=_params(1, 2 * tt * d * (4 + 2) + 6 * tt * d * 4 * int(st.per_row)),
        name="norm_mod",
    )(x, g.reshape(1, d), st.mod_array(scale), st.mod_array(shift))


def _norm_route_body(x_ref, g_ref, sc_ref, sh_ref, wr_ref, br_ref, cin_ref,
                     hn_ref, idx_ref, gate_ref, rank_ref, cnt_ref, carry_ref):
    @pl.when(pl.program_id(0) == 0)
    def _():
        carry_ref[...] = cin_ref[...]

    hn = (_rms(x_ref[...]) * g_ref[...]) * (1.0 + sc_ref[...]) + sh_ref[...]
    hn_ref[...] = hn.astype(hn_ref.dtype)
    logits = lax.dot_general(wr_ref[...], hn, (((1,), (1,)), ((), ())),
                             precision=lax.Precision.HIGHEST,
                             preferred_element_type=F32) + br_ref[...]
    n_e, tt = logits.shape
    e_iota = lax.broadcasted_iota(I32, (n_e, tt), 0)
    vals, sels = [], []
    cur = logits
    for k in range(TOP_K):
        m = jnp.max(cur, axis=0, keepdims=True)
        idx = jnp.min(jnp.where(cur == m, e_iota, n_e), axis=0, keepdims=True)
        sel = e_iota == idx
        idx_ref[k:k + 1, :] = idx
        vals.append(m)
        sels.append(sel)
        cur = jnp.where(sel, -jnp.inf, cur)
    exps = [jnp.exp(v - vals[0]) for v in vals]
    den = exps[0]
    for e in exps[1:]:
        den = den + e
    upper = jnp.where(lax.broadcasted_iota(I32, (tt, tt), 0) <= lax.broadcasted_iota(I32, (tt, tt), 1),
                      1.0, 0.0).astype(BF16)
    base = carry_ref[...]
    for k in range(TOP_K):
        gate_ref[k:k + 1, :] = exps[k] / den
        onehot = jnp.where(sels[k], 1.0, 0.0)
        cum = jnp.dot(onehot.astype(BF16), upper, preferred_element_type=F32)
        rank = jnp.sum(onehot * (cum - 1.0 + base), axis=0, keepdims=True)
        rank_ref[k:k + 1, :] = rank.astype(I32)
        base = base + jnp.sum(onehot, axis=1, keepdims=True)
    carry_ref[...] = base
    cnt_ref[...] = base


def _norm_route(st, x, g, scale, shift, w_router_t, b_router, count_in):
    d = x.shape[1]
    n_e = w_router_t.shape[0]
    tt = st.row_tile(512)
    row_of, col_of = (lambda i: i), (lambda i: 0)
    tok_spec = pl.BlockSpec((TOP_K, tt), lambda i: (0, i))
    vmem = 2 * tt * d * (4 + 2) + 6 * tt * d * 4 * int(st.per_row) + tt * d * 8 + tt * tt * 8
    return pl.pallas_call(
        _norm_route_body,
        grid=(st.rows // tt,),
        in_specs=[pl.BlockSpec((tt, d), lambda i: (i, 0)),
                  pl.BlockSpec((1, d), lambda i: (0, 0)),
                  st.mod_spec(tt, d, row_of, col_of),
                  st.mod_spec(tt, d, row_of, col_of),
                  pl.BlockSpec((n_e, d), lambda i: (0, 0)),
                  pl.BlockSpec((n_e, 1), lambda i: (0, 0)),
                  pl.BlockSpec((n_e, 1), lambda i: (0, 0))],
        out_specs=[pl.BlockSpec((tt, d), lambda i: (i, 0)), tok_spec, tok_spec, tok_spec,
                   pl.BlockSpec((n_e, 1), lambda i: (0, 0))],
        out_shape=[jax.ShapeDtypeStruct((st.rows, d), BF16),
                   jax.ShapeDtypeStruct((TOP_K, st.rows), I32),
                   jax.ShapeDtypeStruct((TOP_K, st.rows), F32),
                   jax.ShapeDtypeStruct((TOP_K, st.rows), I32),
                   jax.ShapeDtypeStruct((n_e, 1), F32)],
        scratch_shapes=[pltpu.VMEM((n_e, 1), F32)],
        compiler_params=_params(1, vmem),
        name="norm_route",
    )(x, g.reshape(1, d), st.mod_array(scale), st.mod_array(shift), w_router_t,
      b_router.reshape(n_e, 1), count_in)


def _mm_body(*refs, n_a, a_fn, n_e, e_fn):
    a_refs = refs[:n_a]
    w_ref = refs[n_a]
    e_refs = refs[n_a + 1:n_a + 1 + n_e]
    o_ref = refs[n_a + 1 + n_e]
    wbf_ref = refs[n_a + 2 + n_e]

    @pl.when(pl.program_id(1) == 0)
    def _():
        wbf_ref[...] = w_ref[...].astype(BF16)

    acc = jnp.dot(a_fn(*a_refs), wbf_ref[...], preferred_element_type=F32)
    e_fn(acc, o_ref, *e_refs)


def _matmul(name, rows, a_ops, a_fn, w_stack, layer, e_ops, e_fn, tm, tn, out_dtype, extra_vmem=0):
    _, k, n = w_stack.shape
    in_specs = ([s for _, s in a_ops]
                + [pl.BlockSpec((None, k, tn), lambda j, i: (layer, 0, j))]
                + [s for _, s in e_ops])
    vmem = 2 * k * tn * 4 + k * tn * 2 + 2 * tm * tn * 4 + extra_vmem
    return pl.pallas_call(
        functools.partial(_mm_body, n_a=len(a_ops), a_fn=a_fn, n_e=len(e_ops), e_fn=e_fn),
        grid=(n // tn, rows // tm),
        in_specs=in_specs,
        out_specs=pl.BlockSpec((tm, tn), lambda j, i: (i, j)),
        out_shape=jax.ShapeDtypeStruct((rows, n), out_dtype),
        scratch_shapes=[pltpu.VMEM((k, tn), BF16)],
        compiler_params=_params(2, vmem),
        name=name,
    )(*[a for a, _ in a_ops], w_stack, *[e for e, _ in e_ops])


def _a_cast(a_ref):
    return a_ref[...].astype(BF16)


def _e_store(acc, o_ref):
    o_ref[...] = acc.astype(o_ref.dtype)


def _e_residual(acc, o_ref, x_ref, gate_ref):
    o_ref[...] = x_ref[...] + gate_ref[...] * acc


def _e_rope(acc, o_ref, cos_ref, sin_ref, *, head_dim):
    is_v = pl.program_id(0) % 3 == 2

    @pl.when(is_v)
    def _():
        o_ref[...] = acc

    @pl.when(jnp.logical_not(is_v))
    def _():
        cos = cos_ref[...]
        sin = sin_ref[...]
        for h in range(acc.shape[1] // head_dim):
            t = acc[:, h * head_dim:(h + 1) * head_dim]
            o_ref[:, h * head_dim:(h + 1) * head_dim] = t * cos + pltpu.roll(t, head_dim // 2, 1) * sin


def _a_merge(o0_ref, o1_ref, o2_ref, l0_ref, l1_ref, l2_ref, *, heads, head_dim):
    lses = [l0_ref[...], l1_ref[...], l2_ref[...]]
    outs = [o0_ref, o1_ref, o2_ref]
    mx = jnp.maximum(jnp.maximum(lses[0], lses[1]), lses[2])
    es = [jnp.exp(l - mx) for l in lses]
    den = es[0] + es[1] + es[2]
    ws = [e / den for e in es]
    cols = []
    for h in range(heads):
        sl = slice(h * head_dim, (h + 1) * head_dim)
        c = ws[0][:, h:h + 1] * outs[0][:, sl]
        c = c + ws[1][:, h:h + 1] * outs[1][:, sl]
        c = c + ws[2][:, h:h + 1] * outs[2][:, sl]
        cols.append(c.astype(BF16))
    return jnp.concatenate(cols, axis=1)


def _band_body(q_ref, kp_ref, kc_ref, vp_ref, vc_ref, o_ref, lse_ref, *, heads, head_dim):
    nk = q_ref.shape[0]
    scale = head_dim ** -0.5
    prev_bias = jnp.where(pl.program_id(2) > 0, 0.0, -jnp.inf)
    qi = lax.broadcasted_iota(I32, (nk, nk), 0)
    kj = lax.broadcasted_iota(I32, (nk, nk), 1)
    lane = lax.broadcasted_iota(I32, (nk, LANES), 1)
    lse_all = jnp.zeros((nk, LANES), F32)
    nt = (((1,), (1,)), ((), ()))
    for h in range(heads):
        sl = slice(h * head_dim, (h + 1) * head_dim)
        q = q_ref[:, sl].astype(BF16)
        sp = lax.dot_general(q, kp_ref[:, sl].astype(BF16), nt, preferred_element_type=F32) * scale
        sc = lax.dot_general(q, kc_ref[:, sl].astype(BF16), nt, preferred_element_type=F32) * scale
        sp = jnp.where(kj >= qi, sp + prev_bias, -jnp.inf)
        sc = jnp.where(kj <= qi, sc, -jnp.inf)
        m = jnp.maximum(jnp.max(sp, axis=-1, keepdims=True), jnp.max(sc, axis=-1, keepdims=True))
        pp = jnp.exp(sp - m)
        pc = jnp.exp(sc - m)
        l = jnp.sum(pp, axis=-1, keepdims=True) + jnp.sum(pc, axis=-1, keepdims=True)
        o = (jnp.dot((pp / l).astype(BF16), vp_ref[:, sl].astype(BF16), preferred_element_type=F32)
             + jnp.dot((pc / l).astype(BF16), vc_ref[:, sl].astype(BF16), preferred_element_type=F32))
        o_ref[:, sl] = o
        lse_all = jnp.where(lane == h, m + jnp.log(l), lse_all)
    lse_ref[...] = lse_all


def _band_attention(qkv, batch, seq, g, heads, head_dim):
    window, dil = DILATED_GROUPS[g]
    nk = window // dil
    hd = heads * head_dim
    n_col = qkv.shape[1] // hd
    assert seq % (nk * dil) == 0, "prompt length must be a multiple of every window span"
    n_blk = seq // dil // nk
    view = qkv.reshape(batch, seq // dil, dil * qkv.shape[1])

    def spec(which, prev):
        def index(b, r, n):
            return (b, jnp.maximum(n - 1, 0) if prev else n, r * n_col + g * 3 + which)
        return pl.BlockSpec((None, nk, hd), index)

    o, lse = pl.pallas_call(
        functools.partial(_band_body, heads=heads, head_dim=head_dim),
        grid=(batch, dil, n_blk),
        in_specs=[spec(0, False), spec(1, True), spec(1, False), spec(2, True), spec(2, False)],
        out_specs=[pl.BlockSpec((None, nk, hd), lambda b, r, n: (b, n, r)),
                   pl.BlockSpec((None, nk, LANES), lambda b, r, n: (b, n, r))],
        out_shape=[jax.ShapeDtypeStruct((batch, seq // dil, dil * hd), F32),
                   jax.ShapeDtypeStruct((batch, seq // dil, dil * LANES), F32)],
        compiler_params=_params(3, 2 * (6 * nk * hd * 4 + nk * LANES * 4)),
        name=f"band_attention_g{g}",
    )(view, view, view, view, view)
    return o.reshape(batch * seq, hd), lse.reshape(batch * seq, LANES)


def _decode_body(qkv_ref, *refs, heads, head_dim, seq):
    cache_refs = refs[:-1]
    o_ref = refs[-1]
    hd = heads * head_dim
    scale = head_dim ** -0.5
    nt = (((1,), (1,)), ((), ()))
    diag = (lax.broadcasted_iota(I32, (heads, hd), 0)
            == lax.broadcasted_iota(I32, (heads, hd), 1) // head_dim)
    for t in range(seq):
        outs, lses = [], []
        for g, (window, dil) in enumerate(DILATED_GROUPS):
            nk = window // dil
            base = g * 3 * hd
            q = jnp.broadcast_to(qkv_ref[t:t + 1, base:base + hd], (heads, hd))
            qbd = jnp.where(diag, q, 0.0).astype(BF16)
            k_new = qkv_ref[:, base + hd:base + 2 * hd].astype(BF16)
            v_new = qkv_ref[:, base + 2 * hd:base + 3 * hd].astype(BF16)
            res, first = t % dil, t // dil
            k_old = cache_refs[2 * g][:, res * hd:(res + 1) * hd].astype(BF16)
            v_old = cache_refs[2 * g + 1][:, res * hd:(res + 1) * hd].astype(BF16)
            s_old = lax.dot_general(qbd, k_old, nt, preferred_element_type=F32) * scale
            s_new = lax.dot_general(qbd, k_new, nt, preferred_element_type=F32) * scale
            mi = lax.broadcasted_iota(I32, (heads, nk), 1)
            ti = lax.broadcasted_iota(I32, (heads, seq), 1)
            s_old = jnp.where(mi >= first, s_old, -jnp.inf)
            s_new = jnp.where(jnp.logical_and(ti <= t, ((t - ti) & (dil - 1)) == 0), s_new, -jnp.inf)
            m = jnp.maximum(jnp.max(s_old, axis=-1, keepdims=True), jnp.max(s_new, axis=-1, keepdims=True))
            p_old = jnp.exp(s_old - m)
            p_new = jnp.exp(s_new - m)
            l = jnp.sum(p_old, axis=-1, keepdims=True) + jnp.sum(p_new, axis=-1, keepdims=True)
            o = (jnp.dot((p_old / l).astype(BF16), v_old, preferred_element_type=F32)
                 + jnp.dot((p_new / l).astype(BF16), v_new, preferred_element_type=F32))
            outs.append(o)
            lses.append(m + jnp.log(l))
        mx = jnp.maximum(jnp.maximum(lses[0], lses[1]), lses[2])
        es = [jnp.exp(l - mx) for l in lses]
        den = es[0] + es[1] + es[2]
        comb = (es[0] / den) * outs[0] + (es[1] / den) * outs[1] + (es[2] / den) * outs[2]
        o_ref[t:t + 1, :] = jnp.sum(jnp.where(diag, comb, 0.0), axis=0, keepdims=True)


def _decode_attention(qkv, caches, layer, batch, seq, heads, head_dim):
    hd = heads * head_dim
    ops, specs = [], []
    vmem = seq * qkv.shape[1] * 4
    for g, (window, dil) in enumerate(DILATED_GROUPS):
        assert dil & (dil - 1) == 0 and seq <= window
        nk = window // dil
        for c in caches[2 * g:2 * g + 2]:
            assert c.shape[2] == window, "the cache must hold exactly one window of past positions"
            n_res = min(dil, seq)
            ops.append(c.reshape(c.shape[0], c.shape[1], nk, dil * hd))
            specs.append(pl.BlockSpec((None, None, nk, n_res * hd), lambda b: (layer, b, 0, 0)))
            vmem += nk * n_res * hd * 4
    return pl.pallas_call(
        functools.partial(_decode_body, heads=heads, head_dim=head_dim, seq=seq),
        grid=(batch,),
        in_specs=[pl.BlockSpec((seq, qkv.shape[1]), lambda b: (b, 0))] + specs,
        out_specs=pl.BlockSpec((seq, hd), lambda b: (b, 0)),
        out_shape=jax.ShapeDtypeStruct((batch * seq, hd), F32),
        compiler_params=_params(1, 2 * vmem),
        name="decode_attention",
    )(qkv, *ops)


def _softplus(z):
    return jnp.maximum(z, 0.0) + jnp.log1p(jnp.exp(-jnp.abs(z)))


def _rg_body(gate_ref, xr_ref, cbuf_ref, cw_ref, cb_ref, wga_ref, bga_ref, wgx_ref, bgx_ref,
             lam_ref, h0_ref, y_ref, hlast_ref, cnew_ref, xs_ref, a_ref, b_ref, h_ref,
             *, tt, n_blocks):
    ti = pl.program_id(1)
    halo = CONV_W - 1
    lo = SUBLANES - halo

    @pl.when(ti == 0)
    def _():
        xs_ref[lo:SUBLANES, :] = cbuf_ref[...]
        h_ref[...] = h0_ref[...]

    @pl.when(ti > 0)
    def _():
        xs_ref[lo:SUBLANES, :] = xs_ref[tt + lo:tt + SUBLANES, :]

    xs_ref[SUBLANES:SUBLANES + tt, :] = xr_ref[...]
    conv = cw_ref[0:1, :] * xs_ref[lo:lo + tt, :]
    for tap in range(1, CONV_W):
        conv = conv + cw_ref[tap:tap + 1, :] * xs_ref[lo + tap:lo + tap + tt, :]
    xc = cb_ref[...] + conv

    @pl.when(ti == pl.num_programs(1) - 1)
    def _():
        cnew_ref[...] = xs_ref[tt + lo:tt + SUBLANES, :]

    xcb = xc.astype(BF16)
    bw = xc.shape[1] // n_blocks
    for n in range(n_blocks):
        sl = slice(n * bw, (n + 1) * bw)
        ga = jnp.dot(xcb[:, sl], wga_ref[n].astype(BF16), preferred_element_type=F32) + bga_ref[:, sl]
        gx = jnp.dot(xcb[:, sl], wgx_ref[n].astype(BF16), preferred_element_type=F32) + bgx_ref[:, sl]
        r = jax.nn.sigmoid(ga)
        i = jax.nn.sigmoid(gx)
        log_a = -LRU_C * r * _softplus(-lam_ref[:, sl])
        a = jnp.exp(log_a)
        a_ref[:, sl] = a
        b_ref[:, sl] = jnp.sqrt(-jnp.tanh(log_a) * (a * a + 1.0)) * (i * xc[:, sl])

    def step(t, h):
        h = a_ref[pl.ds(t, 1), :] * h + b_ref[pl.ds(t, 1), :]
        b_ref[pl.ds(t, 1), :] = h
        return h

    h = lax.fori_loop(0, tt, step, h_ref[...])
    h_ref[...] = h

    @pl.when(ti == pl.num_programs(1) - 1)
    def _():
        hlast_ref[...] = h

    y_ref[...] = (b_ref[...] * jax.nn.gelu(gate_ref[...])).astype(y_ref.dtype)


def _rg_core(st, u, conv_buf, h0, conv_w, conv_b, w_ga, b_ga, w_gx, b_gx, lam, out_dtype):
    dr = u.shape[1] // 2
    n_blocks, bw, _ = w_ga.shape
    batch, seq = st.batch, st.seq
    tt = _pick(seq, (256, 128, 64, 32, 16, 8))
    per = seq // tt
    halo = CONV_W - 1
    vec = lambda: pl.BlockSpec((1, dr), lambda b, t: (0, 0))
    gate_w = lambda: pl.BlockSpec((n_blocks, bw, bw), lambda b, t: (0, 0, 0))
    vmem = 2 * (2 * tt * dr * 4 + 2 * n_blocks * bw * bw * 4 + tt * dr * 4) + 3 * (tt + 8) * dr * 4 + 6 * tt * dr * 4
    y, h_last, c_new = pl.pallas_call(
        functools.partial(_rg_body, tt=tt, n_blocks=n_blocks),
        grid=(batch, per),
        in_specs=[pl.BlockSpec((tt, dr), lambda b, t: (b * per + t, 0)),
                  pl.BlockSpec((tt, dr), lambda b, t: (b * per + t, 1)),
                  pl.BlockSpec((None, halo, dr), lambda b, t: (b, 0, 0)),
                  pl.BlockSpec((CONV_W, dr), lambda b, t: (0, 0)),
                  vec(), gate_w(), vec(), gate_w(), vec(), vec(),
                  pl.BlockSpec((None, 1, dr), lambda b, t: (b, 0, 0))],
        out_specs=[pl.BlockSpec((tt, dr), lambda b, t: (b * per + t, 0)),
                   pl.BlockSpec((None, 1, dr), lambda b, t: (b, 0, 0)),
                   pl.BlockSpec((None, halo, dr), lambda b, t: (b, 0, 0))],
        out_shape=[jax.ShapeDtypeStruct((st.rows, dr), out_dtype),
                   jax.ShapeDtypeStruct((batch, 1, dr), F32),
                   jax.ShapeDtypeStruct((batch, halo, dr), F32)],
        scratch_shapes=[pltpu.VMEM((tt + SUBLANES, dr), F32), pltpu.VMEM((tt, dr), F32),
                        pltpu.VMEM((tt, dr), F32), pltpu.VMEM((1, dr), F32)],
        compiler_params=_params(2, vmem),
        name="rg_core",
    )(u, u, conv_buf, conv_w, conv_b.reshape(1, dr), w_ga, b_ga.reshape(1, dr), w_gx,
      b_gx.reshape(1, dr), lam.reshape(1, dr), h0.reshape(batch, 1, dr))
    return y, h_last.reshape(batch, dr), c_new


def _moe_body(te_ref, tv_ref, nr_ref, x_ref, w1g_ref, w1l_ref, b1g_ref, b1l_ref, w2_ref, b2_ref,
              o_ref, wg_s, wl_s, w2_s):
    i = pl.program_id(0)
    j = pl.program_id(1)

    @pl.when(i < nr_ref[0])
    def _():
        @pl.when(j == 0)
        def _():
            o_ref[...] = jnp.broadcast_to(b2_ref[...], o_ref.shape)

        wg_s[...] = w1g_ref[...].astype(BF16)
        wl_s[...] = w1l_ref[...].astype(BF16)
        w2_s[...] = w2_ref[...].astype(BF16)

        def sub_block(s, carry):
            rows = pl.ds(pl.multiple_of(s * MOE_SUB, MOE_SUB), MOE_SUB)
            xs = x_ref[rows, :]
            glu = jnp.dot(xs, wg_s[...], preferred_element_type=F32) + b1g_ref[...]
            lin = jnp.dot(xs, wl_s[...], preferred_element_type=F32) + b1l_ref[...]
            glu = jnp.minimum(glu, SWIGLU_LIMIT)
            lin = jnp.clip(lin, -SWIGLU_LIMIT, SWIGLU_LIMIT)
            act = glu * jax.nn.sigmoid(SWIGLU_ALPHA * glu) * (lin + 1.0)
            o_ref[rows, :] += jnp.dot(act.astype(BF16), w2_s[...], preferred_element_type=F32)
            return carry

        lax.fori_loop(0, (tv_ref[i] + MOE_SUB - 1) // MOE_SUB, sub_block, 0)


def _moe_experts(x_rows, tile_expert, tile_valid, n_real, w1, b1, w2, b2, layer):
    rows, d = x_rows.shape
    _, n_e, _, two_de = w1.shape
    de = two_de // 2
    tj = _pick(de, (MOE_TJ, 128))
    nj = de // tj
    n_tiles = rows // MOE_TILE

    def tile(i, nr):
        return jnp.minimum(i, nr[0] - 1)

    def col(i, j, nr):
        return jnp.where(i < nr[0], j, nj - 1)

    grid_spec = pltpu.PrefetchScalarGridSpec(
        num_scalar_prefetch=3,
        grid=(n_tiles, nj),
        in_specs=[
            pl.BlockSpec((MOE_TILE, d), lambda i, j, te, tv, nr: (tile(i, nr), 0)),
            pl.BlockSpec((None, None, d, tj), lambda i, j, te, tv, nr: (layer, te[tile(i, nr)], 0, col(i, j, nr))),
            pl.BlockSpec((None, None, d, tj), lambda i, j, te, tv, nr: (layer, te[tile(i, nr)], 0, nj + col(i, j, nr))),
            pl.BlockSpec((None, None, 1, tj), lambda i, j, te, tv, nr: (layer, te[tile(i, nr)], 0, col(i, j, nr))),
            pl.BlockSpec((None, None, 1, tj), lambda i, j, te, tv, nr: (layer, te[tile(i, nr)], 0, nj + col(i, j, nr))),
            pl.BlockSpec((None, None, tj, d), lambda i, j, te, tv, nr: (layer, te[tile(i, nr)], col(i, j, nr), 0)),
            pl.BlockSpec((None, None, 1, d), lambda i, j, te, tv, nr: (layer, te[tile(i, nr)], 0, 0)),
        ],
        out_specs=pl.BlockSpec((MOE_TILE, d), lambda i, j, te, tv, nr: (tile(i, nr), 0)),
        scratch_shapes=[pltpu.VMEM((d, tj), BF16), pltpu.VMEM((d, tj), BF16), pltpu.VMEM((tj, d), BF16)],
    )
    vmem = 2 * MOE_TILE * d * (2 + 4) + 2 * 3 * d * tj * 4 + 3 * d * tj * 2 + 6 * MOE_SUB * tj * 4
    return pl.pallas_call(
        _moe_body,
        grid_spec=grid_spec,
        out_shape=jax.ShapeDtypeStruct((rows, d), F32),
        compiler_params=_params(2, vmem),
        name="moe_experts",
    )(tile_expert, tile_valid, n_real, x_rows, w1, w1,
      b1.reshape(b1.shape[0], n_e, 1, two_de), b1.reshape(b1.shape[0], n_e, 1, two_de),
      w2, b2.reshape(b2.shape[0], n_e, 1, d))


def _combine_body(x_ref, y_ref, gt_ref, g2_ref, o_ref):
    acc = y_ref[0] * gt_ref[:, 0:1]
    for k in range(1, TOP_K):
        acc = acc + y_ref[k] * gt_ref[:, k:k + 1]
    o_ref[...] = x_ref[...] + g2_ref[...] * acc


def _moe_combine(st, x, y_sel, gates_t, gate2):
    d = x.shape[1]
    tt = st.row_tile(256)
    row_of, col_of = (lambda i: i), (lambda i: 0)
    return pl.pallas_call(
        _combine_body,
        grid=(st.rows // tt,),
        in_specs=[pl.BlockSpec((tt, d), lambda i: (i, 0)),
                  pl.BlockSpec((TOP_K, tt, d), lambda i: (0, i, 0)),
                  pl.BlockSpec((tt, TOP_K), lambda i: (i, 0)),
                  st.mod_spec(tt, d, row_of, col_of)],
        out_specs=pl.BlockSpec((tt, d), lambda i: (i, 0)),
        out_shape=jax.ShapeDtypeStruct((st.rows, d), F32),
        compiler_params=_params(1, 2 * tt * d * 4 * (3 + TOP_K)),
        name="moe_combine",
    )(x, y_sel, gates_t, st.mod_array(gate2))


def _moe_layer(streams, xs, mods, g2, w_router, b_router, w1, b1, w2, b2, layer):
    n_e = w_router.shape[-1]
    w_router_t = w_router[layer].T
    counts = jnp.zeros((n_e, 1), F32)
    hns, idxs, gates, ranks = [], [], [], []
    for st, x, (shift2, scale2, _) in zip(streams, xs, mods):
        hn, idx, gate, rank, counts = _norm_route(st, x, g2, scale2, shift2, w_router_t,
                                                  b_router[layer], counts)
        hns.append(hn), idxs.append(idx), gates.append(gate), ranks.append(rank)
    hn_all = jnp.concatenate(hns, axis=0)
    idx_all = jnp.concatenate(idxs, axis=1)
    rank_all = jnp.concatenate(ranks, axis=1)
    n_tok = hn_all.shape[0]

    n_tiles = n_e + (n_tok * TOP_K) // MOE_TILE
    cnt = counts[:, 0].astype(I32)
    tiles_e = (cnt + MOE_TILE - 1) // MOE_TILE
    tile_end = jnp.cumsum(tiles_e)
    tile_start = tile_end - tiles_e
    n_real = tile_end[-1:]
    tile_ids = jnp.arange(n_tiles, dtype=I32)
    tile_expert = jnp.minimum(jnp.searchsorted(tile_end, tile_ids, side='right'), n_e - 1).astype(I32)
    tile_valid = jnp.clip(cnt[tile_expert] - (tile_ids - tile_start[tile_expert]) * MOE_TILE,
                          0, MOE_TILE).astype(I32)
    dest = tile_start[idx_all] * MOE_TILE + rank_all

    tok = jnp.broadcast_to(jnp.arange(n_tok, dtype=I32)[None, :], dest.shape)
    row_src = jnp.full((n_tiles * MOE_TILE,), n_tok, I32).at[dest.reshape(-1)].set(tok.reshape(-1))
    x_rows = jnp.concatenate([hn_all, jnp.zeros((1, hn_all.shape[1]), hn_all.dtype)], axis=0)[row_src]

    y_rows = _moe_experts(x_rows, tile_expert, tile_valid, n_real, w1, b1, w2, b2, layer)

    new_xs, off = [], 0
    for st, x, gate, (_, _, gate2) in zip(streams, xs, gates, mods):
        y_sel = y_rows[dest[:, off:off + st.rows]]
        new_xs.append(_moe_combine(st, x, y_sel, gate.T, gate2))
        off += st.rows
    return new_xs


def _rope_tables(positions, head_dim):
    inv_freq = ROPE_THETA ** (-jnp.arange(0, head_dim, 2, dtype=F32) / head_dim)
    ang = positions.astype(F32)[:, None] * inv_freq[None, :]
    cos, sin = jnp.cos(ang), jnp.sin(ang)
    return jnp.concatenate([cos, cos], axis=-1), jnp.concatenate([-sin, sin], axis=-1)


def kernel(x_prompt, x_sample, c_prompt, c_sample, cache_k_w128, cache_v_w128, cache_k_w512, cache_v_w512, cache_k_w2048, cache_v_w2048, state_rglru_h, state_conv, w_ada, b_ada, g_norm1, g_norm2, w_qkv, w_attn_o, w_rg_in, conv_w, conv_b, w_gate_a, b_gate_a, w_gate_x, b_gate_x, lru_lambda, w_rg_out, w_router, b_router, w_exp1, b_exp1, w_exp2, b_exp2, g_final):
    caches = (cache_k_w128, cache_v_w128, cache_k_w512, cache_v_w512, cache_k_w2048, cache_v_w2048)
    bp, tp, d = x_prompt.shape
    bs, ts, _ = x_sample.shape
    depth = w_ada.shape[0]
    heads, head_dim = cache_k_w128.shape[3], cache_k_w128.shape[4]
    hd = heads * head_dim
    n_groups = len(DILATED_GROUPS)
    dr = w_rg_out.shape[1]
    assert w_qkv.shape[2] == n_groups * 3 * hd

    prompt = _Stream(bp, tp, per_row=False)
    sample = _Stream(bs, ts, per_row=True)
    streams = (prompt, sample)

    n_c = bp + bs
    n_c_pad = -(-n_c // SUBLANES) * SUBLANES
    c_all = jnp.concatenate([c_prompt, c_sample, jnp.zeros((n_c_pad - n_c, d), F32)], axis=0)
    mod = _ada_mod(c_all, w_ada, b_ada).reshape(depth, n_c_pad, 6, d)
    row_range = ((0, bp), (bp, bp + bs))

    cos_p, sin_p = _rope_tables(jnp.arange(tp, dtype=I32), head_dim)
    cos_s, sin_s = _rope_tables(PAST_LEN + jnp.arange(ts, dtype=I32), head_dim)
    rope = ((cos_p, sin_p), (jnp.tile(cos_s, (bs, 1)), jnp.tile(sin_s, (bs, 1))))

    xs = [x_prompt.reshape(bp * tp, d), x_sample.reshape(bs * ts, d)]
    kv_out = [[[] for _ in range(2 * n_groups)] for _ in streams]
    h_out = [[] for _ in streams]
    conv_out = [[] for _ in streams]
    init_h = (None, state_rglru_h)
    init_conv = (None, state_conv)

    for layer in range(depth):
        j = layer // 2
        mods = []
        for si, st in enumerate(streams):
            lo, hi = row_range[si]
            shift1, scale1, gate1, shift2, scale2, gate2 = [mod[layer, lo:hi, n] for n in range(6)]
            mods.append((shift2, scale2, gate2))
            x = xs[si]
            hn = _norm_mod(st, x, g_norm1[layer], scale1, shift1)
            tm = st.row_tile(1024)
            res_ops = lambda tn: [(x, pl.BlockSpec((tm, tn), lambda jj, ii: (ii, jj))),
                                  (st.mod_array(gate1), st.mod_spec(tm, tn, lambda jj, ii: ii, lambda jj, ii: jj))]
            if layer % 2 == 0:
                cos, sin = rope[si]
                per = max(st.seq // tm, 1)
                tab = (pl.BlockSpec((tm, head_dim), lambda jj, ii: (ii % per, 0)) if not st.per_row
                       else pl.BlockSpec((tm, head_dim), lambda jj, ii: (0, 0)))
                qkv = _matmul("qkv_rope", st.rows,
                              [(hn, pl.BlockSpec((tm, d), lambda jj, ii: (ii, 0)))], _a_cast,
                              w_qkv, j, [(cos, tab), (sin, tab)],
                              functools.partial(_e_rope, head_dim=head_dim),
                              tm, hd, F32, extra_vmem=2 * tm * d * 2)
                q5 = qkv.reshape(st.batch, st.seq, n_groups, 3, heads, head_dim)
                for g, (window, _) in enumerate(DILATED_GROUPS):
                    keep = st.seq if st.per_row else min(window, st.seq)
                    kv_out[si][2 * g].append(q5[:, st.seq - keep:, g, 1])
                    kv_out[si][2 * g + 1].append(q5[:, st.seq - keep:, g, 2])
                tn = _pick(d, (1024, 512, 256, 128))
                if not st.per_row:
                    tma = st.row_tile(512)
                    parts = [_band_attention(qkv, st.batch, st.seq, g, heads, head_dim) for g in range(n_groups)]
                    a_ops = ([(o, pl.BlockSpec((tma, hd), lambda jj, ii: (ii, 0))) for o, _ in parts]
                             + [(l, pl.BlockSpec((tma, LANES), lambda jj, ii: (ii, 0))) for _, l in parts])
                    res = [(x, pl.BlockSpec((tma, tn), lambda jj, ii: (ii, jj))),
                           (st.mod_array(gate1), st.mod_spec(tma, tn, lambda jj, ii: ii, lambda jj, ii: jj))]
                    xs[si] = _matmul("attn_out", st.rows, a_ops,
                                     functools.partial(_a_merge, heads=heads, head_dim=head_dim),
                                     w_attn_o, j, res, _e_residual, tma, tn, F32,
                                     extra_vmem=2 * tma * (3 * hd + 3 * LANES + tn) * 4)
                else:
                    comb = _decode_attention(qkv, caches, j, st.batch, st.seq, heads, head_dim)
                    xs[si] = _matmul("attn_out", st.rows,
                                     [(comb, pl.BlockSpec((tm, hd), lambda jj, ii: (ii, 0)))], _a_cast,
                                     w_attn_o, j, res_ops(tn), _e_residual, tm, tn, F32,
                                     extra_vmem=2 * tm * (hd + tn) * 4)
            else:
                tn = _pick(2 * dr, (1024, 512, 256, 128))
                u = _matmul("rg_in", st.rows,
                            [(hn, pl.BlockSpec((tm, d), lambda jj, ii: (ii, 0)))], _a_cast,
                            w_rg_in, j, [], _e_store, tm, tn, F32, extra_vmem=2 * tm * d * 2)
                h0 = jnp.zeros((st.batch, dr), F32) if init_h[si] is None else init_h[si][j]
                cb0 = jnp.zeros((st.batch, CONV_W - 1, dr), F32) if init_conv[si] is None else init_conv[si][j]
                y, h_last, c_new = _rg_core(st, u, cb0, h0, conv_w[j], conv_b[j], w_gate_a[j], b_gate_a[j],
                                            w_gate_x[j], b_gate_x[j], lru_lambda[j],
                                            F32 if st.per_row else BF16)
                h_out[si].append(h_last)
                conv_out[si].append(c_new)
                tn = _pick(d, (1024, 512, 256, 128))
                xs[si] = _matmul("rg_out", st.rows,
                                 [(y, pl.BlockSpec((tm, dr), lambda jj, ii: (ii, 0)))], _a_cast,
                                 w_rg_out, j, res_ops(tn), _e_residual, tm, tn, F32,
                                 extra_vmem=2 * tm * (dr + tn) * 4)
        xs = _moe_layer(streams, xs, mods, g_norm2[layer], w_router, b_router,
                        w_exp1, b_exp1, w_exp2, b_exp2, layer)

    ys = [_final_norm(st, x, g_final).reshape(st.batch, st.seq, d) for st, x in zip(streams, xs)]
    outs = []
    for si in range(2):
        outs.append([jnp.stack(a, axis=0) for a in kv_out[si]]
                    + [jnp.stack(h_out[si], axis=0), jnp.stack(conv_out[si], axis=0)])
    return (ys[0], ys[1], *outs[0], *outs[1])
```

```python
import functools

import numpy as np
import jax
import jax.numpy as jnp
from jax import lax
from jax.experimental import pallas as pl
from jax.experimental.pallas import tpu as pltpu

F32 = jnp.float32
BF16 = jnp.bfloat16
I32 = jnp.int32

DILATED_GROUPS = ((128, 1), (512, 4), (2048, 16))
ROPE_THETA = 10000.0
PAST_LEN = 16384
CONV_W = 4
LRU_C = 8.0
TOP_K = 4
SWIGLU_LIMIT = 7.0
SWIGLU_ALPHA = 1.702
NORM_EPS = 1e-6

V7X_VMEM_BYTES = 64 * 1024 * 1024
VMEM_LIMIT_CAP = 56 * 1024 * 1024
LANES = 128
SUBLANES = 8

MOE_TILE = 1152
MOE_TJ = 256
MOE_ROW_SPLIT = 4
U32 = jnp.uint32


def _pack_halves(x):
    half = x.shape[1] // 2
    lo = lax.bitcast_convert_type(x[:, :half].astype(BF16).astype(F32), U32)
    hi = lax.bitcast_convert_type(x[:, half:].astype(BF16).astype(F32), U32)
    return (lo >> 16) | hi


def _unpack_halves(w):
    lo = lax.bitcast_convert_type(w << 16, F32).astype(BF16)
    hi = lax.bitcast_convert_type(w & U32(0xFFFF0000), F32).astype(BF16)
    return lo, hi


def _params(n_axes, vmem_bytes):
    limit = int(min(VMEM_LIMIT_CAP, max(vmem_bytes * 5 // 4 + (4 << 20), 16 << 20)))
    return pltpu.CompilerParams(dimension_semantics=("arbitrary",) * n_axes,
                                vmem_limit_bytes=limit)


def _pick(n, candidates):
    for c in candidates:
        if n % c == 0:
            return c
    return n


class _Stream:
    def __init__(self, batch, seq, per_row):
        self.batch, self.seq, self.per_row = batch, seq, per_row
        self.rows = batch * seq

    def mod_array(self, vec):
        if self.per_row:
            return jnp.repeat(vec, self.seq, axis=0)
        return vec[:, None, :]

    def mod_spec(self, tm, tn, row_of, col_of):
        if self.per_row:
            return pl.BlockSpec((tm, tn), lambda *g: (row_of(*g), col_of(*g)))
        per_batch = self.seq // tm
        return pl.BlockSpec((None, 1, tn), lambda *g: (row_of(*g) // per_batch, 0, col_of(*g)))

    def row_tile(self, cap):
        if self.per_row:
            return self.rows
        return _pick(self.seq, [c for c in (1024, 512, 256, 128, 64, 32, 16, 8) if c <= cap])


def _ada_body(c_ref, w_ref, b_ref, o_ref):
    c = c_ref[...]
    s = (c * jax.nn.sigmoid(c)).astype(BF16)
    o_ref[0] = jnp.dot(s, w_ref[0].astype(BF16), preferred_element_type=F32) + b_ref[0]


def _ada_mod(c_all, w_ada, b_ada):
    n_layers, d, n6 = w_ada.shape
    mc = c_all.shape[0]
    tn = _pick(n6, (1024, 512, 256, 128))
    vmem = 2 * (mc * d * 4 + d * tn * 4 + tn * 4 + mc * tn * 4) + d * tn * 2
    return pl.pallas_call(
        _ada_body,
        grid=(n_layers, n6 // tn),
        in_specs=[pl.BlockSpec((mc, d), lambda l, j: (0, 0)),
                  pl.BlockSpec((1, d, tn), lambda l, j: (l, 0, j)),
                  pl.BlockSpec((1, 1, tn), lambda l, j: (l, 0, j))],
        out_specs=pl.BlockSpec((1, mc, tn), lambda l, j: (l, 0, j)),
        out_shape=jax.ShapeDtypeStruct((n_layers, mc, n6), F32),
        compiler_params=_params(2, vmem),
        name="ada_mod",
    )(c_all, w_ada, b_ada.reshape(n_layers, 1, n6))


def _rms(x):
    return x * lax.rsqrt(jnp.mean(x * x, axis=-1, keepdims=True) + NORM_EPS)


def _norm_body(x_ref, g_ref, o_ref):
    o_ref[...] = (_rms(x_ref[...]) * g_ref[...]).astype(o_ref.dtype)


def _norm_mod_body(x_ref, g_ref, sc_ref, sh_ref, o_ref):
    hn = (_rms(x_ref[...]) * g_ref[...]) * (1.0 + sc_ref[...]) + sh_ref[...]
    o_ref[...] = hn.astype(o_ref.dtype)


def _final_norm(st, x, g):
    d = x.shape[1]
    tt = st.row_tile(512)
    return pl.pallas_call(
        _norm_body,
        grid=(st.rows // tt,),
        in_specs=[pl.BlockSpec((tt, d), lambda i: (i, 0)),
                  pl.BlockSpec((1, d), lambda i: (0, 0))],
        out_specs=pl.BlockSpec((tt, d), lambda i: (i, 0)),
        out_shape=jax.ShapeDtypeStruct((st.rows, d), F32),
        compiler_params=_params(1, 4 * tt * d * 4),
        name="final_norm",
    )(x, g.reshape(1, d))


def _norm_mod(st, x, g, scale, shift):
    d = x.shape[1]
    tt = st.row_tile(512)
    row_of, col_of = (lambda i: i), (lambda i: 0)
    return pl.pallas_call(
        _norm_mod_body,
        grid=(st.rows // tt,),
        in_specs=[pl.BlockSpec((tt, d), lambda i: (i, 0)),
                  pl.BlockSpec((1, d), lambda i: (0, 0)),
                  st.mod_spec(tt, d, row_of, col_of),
                  st.mod_spec(tt, d, row_of, col_of)],
        out_specs=pl.BlockSpec((tt, d), lambda i: (i, 0)),
        out_shape=jax.ShapeDtypeStruct((st.rows, d), BF16),
        compiler_params=_params(1, 2 * tt * d * (4 + 2) + 6 * tt * d * 4 * int(st.per_row)),
        name="norm_mod",
    )(x, g.reshape(1, d), st.mod_array(scale), st.mod_array(shift))


def _norm_route_body(x_ref, g_ref, sc_ref, sh_ref, wr_ref, br_ref, cin_ref,
                     hn_ref, idx_ref, gate_ref, rank_ref, cnt_ref, carry_ref):
    @pl.when(pl.program_id(0) == 0)
    def _():
        carry_ref[...] = cin_ref[...]

    hn = (_rms(x_ref[...]) * g_ref[...]) * (1.0 + sc_ref[...]) + sh_ref[...]
    hn_ref[...] = _pack_halves(hn)
    logits = lax.dot_general(wr_ref[...], hn, (((1,), (1,)), ((), ())),
                             precision=lax.Precision.HIGHEST,
                             preferred_element_type=F32) + br_ref[...]
    n_e, tt = logits.shape
    e_iota = lax.broadcasted_iota(I32, (n_e, tt), 0)
    vals, sels = [], []
    cur = logits
    for k in range(TOP_K):
        m = jnp.max(cur, axis=0, keepdims=True)
        idx = jnp.min(jnp.where(cur == m, e_iota, n_e), axis=0, keepdims=True)
        sel = e_iota == idx
        idx_ref[k:k + 1, :] = idx
        vals.append(m)
        sels.append(sel)
        cur = jnp.where(sel, -jnp.inf, cur)
    exps = [jnp.exp(v - vals[0]) for v in vals]
    den = exps[0]
    for e in exps[1:]:
        den = den + e
    upper = jnp.where(lax.broadcasted_iota(I32, (tt, tt), 0) <= lax.broadcasted_iota(I32, (tt, tt), 1),
                      1.0, 0.0).astype(BF16)
    base = carry_ref[...]
    for k in range(TOP_K):
        gate_ref[k:k + 1, :] = exps[k] / den
        onehot = jnp.where(sels[k], 1.0, 0.0)
        cum = jnp.dot(onehot.astype(BF16), upper, preferred_element_type=F32)
        rank = jnp.sum(onehot * (cum - 1.0 + base), axis=0, keepdims=True)
        rank_ref[k:k + 1, :] = rank.astype(I32)
        base = base + jnp.sum(onehot, axis=1, keepdims=True)
    carry_ref[...] = base
    cnt_ref[...] = base


def _norm_route(st, x, g, scale, shift, w_router_t, b_router, count_in):
    d = x.shape[1]
    n_e = w_router_t.shape[0]
    tt = st.row_tile(512)
    row_of, col_of = (lambda i: i), (lambda i: 0)
    tok_spec = pl.BlockSpec((TOP_K, tt), lambda i: (0, i))
    vmem = 2 * tt * d * (4 + 2) + 6 * tt * d * 4 * int(st.per_row) + tt * d * 8 + tt * tt * 8
    return pl.pallas_call(
        _norm_route_body,
        grid=(st.rows // tt,),
        in_specs=[pl.BlockSpec((tt, d), lambda i: (i, 0)),
                  pl.BlockSpec((1, d), lambda i: (0, 0)),
                  st.mod_spec(tt, d, row_of, col_of),
                  st.mod_spec(tt, d, row_of, col_of),
                  pl.BlockSpec((n_e, d), lambda i: (0, 0)),
                  pl.BlockSpec((n_e, 1), lambda i: (0, 0)),
                  pl.BlockSpec((n_e, 1), lambda i: (0, 0))],
        out_specs=[pl.BlockSpec((tt, d // 2), lambda i: (i, 0)), tok_spec, tok_spec, tok_spec,
                   pl.BlockSpec((n_e, 1), lambda i: (0, 0))],
        out_shape=[jax.ShapeDtypeStruct((st.rows, d // 2), U32),
                   jax.ShapeDtypeStruct((TOP_K, st.rows), I32),
                   jax.ShapeDtypeStruct((TOP_K, st.rows), F32),
                   jax.ShapeDtypeStruct((TOP_K, st.rows), I32),
                   jax.ShapeDtypeStruct((n_e, 1), F32)],
        scratch_shapes=[pltpu.VMEM((n_e, 1), F32)],
        compiler_params=_params(1, vmem),
        name="norm_route",
    )(x, g.reshape(1, d), st.mod_array(scale), st.mod_array(shift), w_router_t,
      b_router.reshape(n_e, 1), count_in)


def _mm_body(*refs, n_a, a_fn, n_e, e_fn):
    a_refs = refs[:n_a]
    w_ref = refs[n_a]
    e_refs = refs[n_a + 1:n_a + 1 + n_e]
    o_ref = refs[n_a + 1 + n_e]
    wbf_ref = refs[n_a + 2 + n_e]

    @pl.when(pl.program_id(1) == 0)
    def _():
        wbf_ref[...] = w_ref[...].astype(BF16)

    acc = jnp.dot(a_fn(*a_refs), wbf_ref[...], preferred_element_type=F32)
    e_fn(acc, o_ref, *e_refs)


def _matmul(name, rows, a_ops, a_fn, w_stack, layer, e_ops, e_fn, tm, tn, out_dtype, extra_vmem=0):
    _, k, n = w_stack.shape
    in_specs = ([s for _, s in a_ops]
                + [pl.BlockSpec((None, k, tn), lambda j, i: (layer, 0, j))]
                + [s for _, s in e_ops])
    vmem = 2 * k * tn * 4 + k * tn * 2 + 2 * tm * tn * 4 + extra_vmem
    return pl.pallas_call(
        functools.partial(_mm_body, n_a=len(a_ops), a_fn=a_fn, n_e=len(e_ops), e_fn=e_fn),
        grid=(n // tn, rows // tm),
        in_specs=in_specs,
        out_specs=pl.BlockSpec((tm, tn), lambda j, i: (i, j)),
        out_shape=jax.ShapeDtypeStruct((rows, n), out_dtype),
        scratch_shapes=[pltpu.VMEM((k, tn), BF16)],
        compiler_params=_params(2, vmem),
        name=name,
    )(*[a for a, _ in a_ops], w_stack, *[e for e, _ in e_ops])


def _a_cast(a_ref):
    return a_ref[...].astype(BF16)


def _e_store(acc, o_ref):
    o_ref[...] = acc.astype(o_ref.dtype)


def _e_residual(acc, o_ref, x_ref, gate_ref):
    o_ref[...] = x_ref[...] + gate_ref[...] * acc


def _e_rope(acc, o_ref, cos_ref, sin_ref, *, head_dim):
    is_v = pl.program_id(0) % 3 == 2

    @pl.when(is_v)
    def _():
        o_ref[...] = acc

    @pl.when(jnp.logical_not(is_v))
    def _():
        cos = cos_ref[...]
        sin = sin_ref[...]
        for h in range(acc.shape[1] // head_dim):
            t = acc[:, h * head_dim:(h + 1) * head_dim]
            o_ref[:, h * head_dim:(h + 1) * head_dim] = t * cos + pltpu.roll(t, head_dim // 2, 1) * sin


def _a_merge(o0_ref, o1_ref, o2_ref, l0_ref, l1_ref, l2_ref, *, heads, head_dim):
    lses = [l0_ref[...], l1_ref[...], l2_ref[...]]
    outs = [o0_ref, o1_ref, o2_ref]
    mx = jnp.maximum(jnp.maximum(lses[0], lses[1]), lses[2])
    es = [jnp.exp(l - mx) for l in lses]
    den = es[0] + es[1] + es[2]
    ws = [e / den for e in es]
    cols = []
    for h in range(heads):
        sl = slice(h * head_dim, (h + 1) * head_dim)
        c = ws[0][:, h:h + 1] * outs[0][:, sl]
        c = c + ws[1][:, h:h + 1] * outs[1][:, sl]
        c = c + ws[2][:, h:h + 1] * outs[2][:, sl]
        cols.append(c.astype(BF16))
    return jnp.concatenate(cols, axis=1)


def _attend(q, k_cat, v_cat, prev_bias, scale):
    nk = q.shape[0]
    s = lax.dot_general(q, k_cat, (((1,), (1,)), ((), ())), preferred_element_type=F32) * scale
    qi = lax.broadcasted_iota(I32, (nk, 2 * nk), 0)
    kj = lax.broadcasted_iota(I32, (nk, 2 * nk), 1)
    s = s + jnp.where(kj < nk, prev_bias, 0.0)
    s = jnp.where(jnp.logical_and(kj >= qi, kj <= qi + nk), s, -jnp.inf)
    m = jnp.max(s, axis=-1, keepdims=True)
    p = jnp.exp(s - m)
    l = jnp.sum(p, axis=-1, keepdims=True)
    o = jnp.dot((p / l).astype(BF16), v_cat, preferred_element_type=F32)
    return o, m + jnp.log(l)


def _band_dense_body(q_ref, kp_ref, kc_ref, vp_ref, vc_ref, o_ref, lse_ref, *, heads, head_dim, nk):
    scale = head_dim ** -0.5
    first_bias = jnp.where(pl.program_id(1) > 0, 0.0, -jnp.inf)
    lane = lax.broadcasted_iota(I32, (nk, LANES), 1)
    for s in range(q_ref.shape[0] // nk):
        rows = slice(s * nk, (s + 1) * nk)
        lse_all = jnp.zeros((nk, LANES), F32)
        for h in range(heads):
            sl = slice(h * head_dim, (h + 1) * head_dim)
            q = q_ref[rows, sl].astype(BF16)
            if s == 0:
                k_cat = jnp.concatenate([kp_ref[:, sl], kc_ref[rows, sl]], axis=0).astype(BF16)
                v_cat = jnp.concatenate([vp_ref[:, sl], vc_ref[rows, sl]], axis=0).astype(BF16)
                bias = first_bias
            else:
                k_cat = kc_ref[(s - 1) * nk:(s + 1) * nk, sl].astype(BF16)
                v_cat = vc_ref[(s - 1) * nk:(s + 1) * nk, sl].astype(BF16)
                bias = 0.0
            o, lse = _attend(q, k_cat, v_cat, bias, scale)
            o_ref[rows, sl] = o
            lse_all = jnp.where(lane == h, lse, lse_all)
        lse_ref[rows, :] = lse_all


def _band_span_body(q_ref, kp_ref, kc_ref, vp_ref, vc_ref, o_ref, lse_ref, *, dil, heads_per_step, head_dim):
    scale = head_dim ** -0.5
    nk = q_ref.shape[0] // dil
    head0 = pl.program_id(2) * heads_per_step
    prev_bias = jnp.where(pl.program_id(1) > 0, 0.0, -jnp.inf)
    lane = lax.broadcasted_iota(I32, (nk, LANES), 1)

    @pl.when(pl.program_id(2) == 0)
    def _():
        lse_ref[...] = jnp.zeros(lse_ref.shape, F32)

    for r in range(dil):
        rows = pl.ds(r, nk, stride=dil)
        lse_all = lse_ref[rows, :]
        for h in range(heads_per_step):
            sl = slice(h * head_dim, (h + 1) * head_dim)
            q = q_ref[rows, sl].astype(BF16)
            k_cat = jnp.concatenate([kp_ref[rows, sl], kc_ref[rows, sl]], axis=0).astype(BF16)
            v_cat = jnp.concatenate([vp_ref[rows, sl], vc_ref[rows, sl]], axis=0).astype(BF16)
            o, lse = _attend(q, k_cat, v_cat, prev_bias, scale)
            o_ref[rows, sl] = o
            lse_all = jnp.where(lane == head0 + h, lse, lse_all)
        lse_ref[rows, :] = lse_all


def _band_attention(qkv, batch, seq, g, heads, head_dim):
    window, dil = DILATED_GROUPS[g]
    nk = window // dil
    hd = heads * head_dim
    rows = batch * seq
    assert seq % window == 0, "prompt length must be a multiple of every window span"
    out_shape = [jax.ShapeDtypeStruct((rows, hd), F32), jax.ShapeDtypeStruct((rows, LANES), F32)]
    if dil == 1:
        tq = _pick(seq, (4 * nk, 2 * nk, nk))
        per, sub = seq // tq, tq // nk
        cur = lambda which: pl.BlockSpec((tq, hd), lambda b, n: (b * per + n, g * 3 + which))
        prev = lambda which: pl.BlockSpec(
            (nk, hd), lambda b, n: (b * per * sub + jnp.maximum(n * sub - 1, 0), g * 3 + which))
        return pl.pallas_call(
            functools.partial(_band_dense_body, heads=heads, head_dim=head_dim, nk=nk),
            grid=(batch, per),
            in_specs=[cur(0), prev(1), cur(1), prev(2), cur(2)],
            out_specs=[pl.BlockSpec((tq, hd), lambda b, n: (b * per + n, 0)),
                       pl.BlockSpec((tq, LANES), lambda b, n: (b * per + n, 0))],
            out_shape=out_shape,
            compiler_params=_params(2, 2 * (4 * tq + 2 * nk) * hd * 4 + (8 << 20)),
            name=f"band_attention_g{g}",
        )(qkv, qkv, qkv, qkv, qkv)
    assert head_dim == LANES
    hps = 1
    n_hs = heads // hps
    per = seq // window
    wide = hps * head_dim

    def spec(which, back):
        return pl.BlockSpec((window, wide),
                            lambda b, n, hs: (b * per + jnp.maximum(n - back, 0), (g * 3 + which) * n_hs + hs))

    return pl.pallas_call(
        functools.partial(_band_span_body, dil=dil, heads_per_step=hps, head_dim=head_dim),
        grid=(batch, per, n_hs),
        in_specs=[spec(0, 0), spec(1, 1), spec(1, 0), spec(2, 1), spec(2, 0)],
        out_specs=[pl.BlockSpec((window, wide), lambda b, n, hs: (b * per + n, hs)),
                   pl.BlockSpec((window, LANES), lambda b, n, hs: (b * per + n, 0))],
        out_shape=out_shape,
        compiler_params=_params(3, 2 * (6 * window * wide + window * LANES) * 4 + (8 << 20)),
        name=f"band_attention_g{g}",
    )(qkv, qkv, qkv, qkv, qkv)


def _decode_body(qkv_ref, *refs, heads, head_dim, seq):
    cache_refs = refs[:-1]
    o_ref = refs[-1]
    hd = heads * head_dim
    scale = head_dim ** -0.5
    nt = (((1,), (1,)), ((), ()))
    diag = (lax.broadcasted_iota(I32, (heads, hd), 0)
            == lax.broadcasted_iota(I32, (heads, hd), 1) // head_dim)
    for t in range(seq):
        outs, lses = [], []
        for g, (window, dil) in enumerate(DILATED_GROUPS):
            nk = window // dil
            base = g * 3 * hd
            q = jnp.broadcast_to(qkv_ref[t:t + 1, base:base + hd], (heads, hd))
            qbd = jnp.where(diag, q, 0.0).astype(BF16)
            k_new = qkv_ref[:, base + hd:base + 2 * hd].astype(BF16)
            v_new = qkv_ref[:, base + 2 * hd:base + 3 * hd].astype(BF16)
            res, first = t % dil, t // dil
            k_old = cache_refs[2 * g][:, res * hd:(res + 1) * hd].astype(BF16)
            v_old = cache_refs[2 * g + 1][:, res * hd:(res + 1) * hd].astype(BF16)
            s_old = lax.dot_general(qbd, k_old, nt, preferred_element_type=F32) * scale
            s_new = lax.dot_general(qbd, k_new, nt, preferred_element_type=F32) * scale
            mi = lax.broadcasted_iota(I32, (heads, nk), 1)
            ti = lax.broadcasted_iota(I32, (heads, seq), 1)
            s_old = jnp.where(mi >= first, s_old, -jnp.inf)
            s_new = jnp.where(jnp.logical_and(ti <= t, ((t - ti) & (dil - 1)) == 0), s_new, -jnp.inf)
            m = jnp.maximum(jnp.max(s_old, axis=-1, keepdims=True), jnp.max(s_new, axis=-1, keepdims=True))
            p_old = jnp.exp(s_old - m)
            p_new = jnp.exp(s_new - m)
            l = jnp.sum(p_old, axis=-1, keepdims=True) + jnp.sum(p_new, axis=-1, keepdims=True)
            o = (jnp.dot((p_old / l).astype(BF16), v_old, preferred_element_type=F32)
                 + jnp.dot((p_new / l).astype(BF16), v_new, preferred_element_type=F32))
            outs.append(o)
            lses.append(m + jnp.log(l))
        mx = jnp.maximum(jnp.maximum(lses[0], lses[1]), lses[2])
        es = [jnp.exp(l - mx) for l in lses]
        den = es[0] + es[1] + es[2]
        comb = (es[0] / den) * outs[0] + (es[1] / den) * outs[1] + (es[2] / den) * outs[2]
        o_ref[t:t + 1, :] = jnp.sum(jnp.where(diag, comb, 0.0), axis=0, keepdims=True)


def _decode_attention(qkv, caches, layer, batch, seq, heads, head_dim):
    hd = heads * head_dim
    ops, specs = [], []
    vmem = seq * qkv.shape[1] * 4
    for g, (window, dil) in enumerate(DILATED_GROUPS):
        assert dil & (dil - 1) == 0 and seq <= window
        nk = window // dil
        for c in caches[2 * g:2 * g + 2]:
            assert c.shape[2] == window, "the cache must hold exactly one window of past positions"
            n_res = min(dil, seq)
            ops.append(c.reshape(c.shape[0], c.shape[1], nk, dil * hd))
            specs.append(pl.BlockSpec((None, None, nk, n_res * hd), lambda b: (layer, b, 0, 0)))
            vmem += nk * n_res * hd * 4
    return pl.pallas_call(
        functools.partial(_decode_body, heads=heads, head_dim=head_dim, seq=seq),
        grid=(batch,),
        in_specs=[pl.BlockSpec((seq, qkv.shape[1]), lambda b: (b, 0))] + specs,
        out_specs=pl.BlockSpec((seq, hd), lambda b: (b, 0)),
        out_shape=jax.ShapeDtypeStruct((batch * seq, hd), F32),
        compiler_params=_params(1, 2 * vmem),
        name="decode_attention",
    )(qkv, *ops)


def _softplus(z):
    return jnp.maximum(z, 0.0) + jnp.log1p(jnp.exp(-jnp.abs(z)))


def _rg_body(gate_ref, xr_ref, cbuf_ref, cw_ref, cb_ref, wga_ref, bga_ref, wgx_ref, bgx_ref,
             lam_ref, h0_ref, y_ref, hlast_ref, cnew_ref, xs_ref, a_ref, b_ref, h_ref,
             *, tt, n_blocks):
    ti = pl.program_id(1)
    halo = CONV_W - 1
    lo = SUBLANES - halo

    @pl.when(ti == 0)
    def _():
        xs_ref[lo:SUBLANES, :] = cbuf_ref[...]
        h_ref[...] = h0_ref[...]

    @pl.when(ti > 0)
    def _():
        xs_ref[lo:SUBLANES, :] = xs_ref[tt + lo:tt + SUBLANES, :]

    xs_ref[SUBLANES:SUBLANES + tt, :] = xr_ref[...]
    conv = cw_ref[0:1, :] * xs_ref[lo:lo + tt, :]
    for tap in range(1, CONV_W):
        conv = conv + cw_ref[tap:tap + 1, :] * xs_ref[lo + tap:lo + tap + tt, :]
    xc = cb_ref[...] + conv

    @pl.when(ti == pl.num_programs(1) - 1)
    def _():
        cnew_ref[...] = xs_ref[tt + lo:tt + SUBLANES, :]

    xcb = xc.astype(BF16)
    bw = xc.shape[1] // n_blocks
    for n in range(n_blocks):
        sl = slice(n * bw, (n + 1) * bw)
        ga = jnp.dot(xcb[:, sl], wga_ref[n].astype(BF16), preferred_element_type=F32) + bga_ref[:, sl]
        gx = jnp.dot(xcb[:, sl], wgx_ref[n].astype(BF16), preferred_element_type=F32) + bgx_ref[:, sl]
        r = jax.nn.sigmoid(ga)
        i = jax.nn.sigmoid(gx)
        log_a = -LRU_C * r * _softplus(-lam_ref[:, sl])
        a = jnp.exp(log_a)
        a_ref[:, sl] = a
        b_ref[:, sl] = jnp.sqrt(-jnp.tanh(log_a) * (a * a + 1.0)) * (i * xc[:, sl])

    def step(t, h):
        h = a_ref[pl.ds(t, 1), :] * h + b_ref[pl.ds(t, 1), :]
        b_ref[pl.ds(t, 1), :] = h
        return h

    h = lax.fori_loop(0, tt, step, h_ref[...])
    h_ref[...] = h

    @pl.when(ti == pl.num_programs(1) - 1)
    def _():
        hlast_ref[...] = h

    y_ref[...] = (b_ref[...] * jax.nn.gelu(gate_ref[...])).astype(y_ref.dtype)


def _rg_core(st, u, conv_buf, h0, conv_w, conv_b, w_ga, b_ga, w_gx, b_gx, lam, out_dtype):
    dr = u.shape[1] // 2
    n_blocks, bw, _ = w_ga.shape
    batch, seq = st.batch, st.seq
    tt = _pick(seq, (256, 128, 64, 32, 16, 8))
    per = seq // tt
    halo = CONV_W - 1
    vec = lambda: pl.BlockSpec((1, dr), lambda b, t: (0, 0))
    gate_w = lambda: pl.BlockSpec((n_blocks, bw, bw), lambda b, t: (0, 0, 0))
    vmem = 2 * (2 * tt * dr * 4 + 2 * n_blocks * bw * bw * 4 + tt * dr * 4) + 3 * (tt + 8) * dr * 4 + 6 * tt * dr * 4
    y, h_last, c_new = pl.pallas_call(
        functools.partial(_rg_body, tt=tt, n_blocks=n_blocks),
        grid=(batch, per),
        in_specs=[pl.BlockSpec((tt, dr), lambda b, t: (b * per + t, 0)),
                  pl.BlockSpec((tt, dr), lambda b, t: (b * per + t, 1)),
                  pl.BlockSpec((None, halo, dr), lambda b, t: (b, 0, 0)),
                  pl.BlockSpec((CONV_W, dr), lambda b, t: (0, 0)),
                  vec(), gate_w(), vec(), gate_w(), vec(), vec(),
                  pl.BlockSpec((None, 1, dr), lambda b, t: (b, 0, 0))],
        out_specs=[pl.BlockSpec((tt, dr), lambda b, t: (b * per + t, 0)),
                   pl.BlockSpec((None, 1, dr), lambda b, t: (b, 0, 0)),
                   pl.BlockSpec((None, halo, dr), lambda b, t: (b, 0, 0))],
        out_shape=[jax.ShapeDtypeStruct((st.rows, dr), out_dtype),
                   jax.ShapeDtypeStruct((batch, 1, dr), F32),
                   jax.ShapeDtypeStruct((batch, halo, dr), F32)],
        scratch_shapes=[pltpu.VMEM((tt + SUBLANES, dr), F32), pltpu.VMEM((tt, dr), F32),
                        pltpu.VMEM((tt, dr), F32), pltpu.VMEM((1, dr), F32)],
        compiler_params=_params(2, vmem),
        name="rg_core",
    )(u, u, conv_buf, conv_w, conv_b.reshape(1, dr), w_ga, b_ga.reshape(1, dr), w_gx,
      b_gx.reshape(1, dr), lam.reshape(1, dr), h0.reshape(batch, 1, dr))
    return y, h_last.reshape(batch, dr), c_new


def _moe_body(te_ref, tv_ref, tb_ref, nr_ref, tok_ref, dst_ref,
              hn_hbm, w1g_ref, w1l_ref, b1g_ref, b1l_ref, w2_ref, b2_ref, y_hbm,
              xbuf, acc, wg_s, wl_s, w2_s, gsem, ssem, *, n_tiles, nj):
    i = pl.program_id(0)
    j = pl.program_id(1)
    n_real = nr_ref[0]
    slot = i % 2
    other = 1 - slot
    tile = xbuf.shape[1]
    per_step = tile // nj
    last = n_real - 1

    def start_gather(t, s, r):
        tok = tok_ref[tb_ref[t] + r]
        pltpu.make_async_copy(hn_hbm.at[pl.ds(tok, 1)], xbuf.at[s, pl.ds(r, 1)], gsem.at[s]).start()

    def wait_gathers(s):
        pltpu.make_async_copy(hn_hbm.at[pl.ds(0, tile)], xbuf.at[s], gsem.at[s]).wait()

    def start_scatter(t, s, r):
        dst = dst_ref[tb_ref[t] + r]
        pltpu.make_async_copy(acc.at[s, pl.ds(r, 1)], y_hbm.at[pl.ds(dst, 1)], ssem.at[s]).start()

    def wait_scatters(t, s, enabled):
        n = tv_ref[t]
        rows = 1
        while rows <= tile:
            @pl.when(jnp.logical_and(enabled, (n & rows) != 0))
            def _():
                pltpu.make_async_copy(acc.at[s, pl.ds(0, rows)], y_hbm.at[pl.ds(0, rows)], ssem.at[s]).wait()
            rows *= 2

    @pl.when(i < n_real)
    def _():
        @pl.when(j == 0)
        def _():
            @pl.when(i == 0)
            def _():
                def first_rows(r, carry):
                    start_gather(0, 0, r)
                    return carry
                lax.fori_loop(0, tile, first_rows, 0)

            wait_gathers(slot)
            wait_scatters(jnp.maximum(i - 2, 0), slot, i >= 2)
            acc[slot] = jnp.broadcast_to(b2_ref[...], acc.shape[1:])

        nxt = jnp.minimum(i + 1, last)
        prev = jnp.maximum(i - 1, 0)
        for r in range(per_step):
            start_gather(nxt, other, j * per_step + r)
        for r in range(per_step):
            rr = j * per_step + r

            @pl.when(jnp.logical_and(i >= 1, rr < tv_ref[prev]))
            def _():
                start_scatter(prev, other, rr)

        wg_s[...] = w1g_ref[...].astype(BF16)
        wl_s[...] = w1l_ref[...].astype(BF16)
        w2_s[...] = w2_ref[...].astype(BF16)
        half = wg_s.shape[0] // 2
        part = tile // MOE_ROW_SPLIT
        for p in range(MOE_ROW_SPLIT):
            rows = slice(p * part, (p + 1) * part)
            lo, hi = _unpack_halves(xbuf[slot, rows, :])
            glu = (jnp.dot(lo, wg_s[:half, :], preferred_element_type=F32)
                   + jnp.dot(hi, wg_s[half:, :], preferred_element_type=F32)) + b1g_ref[...]
            lin = (jnp.dot(lo, wl_s[:half, :], preferred_element_type=F32)
                   + jnp.dot(hi, wl_s[half:, :], preferred_element_type=F32)) + b1l_ref[...]
            glu = jnp.minimum(glu, SWIGLU_LIMIT)
            lin = jnp.clip(lin, -SWIGLU_LIMIT, SWIGLU_LIMIT)
            act = glu * jax.nn.sigmoid(SWIGLU_ALPHA * glu) * (lin + 1.0)
            acc[slot, rows, :] = acc[slot, rows, :] + jnp.dot(act.astype(BF16), w2_s[...],
                                                              preferred_element_type=F32)

    @pl.when(jnp.logical_and(i == n_tiles - 1, j == nj - 1))
    def _():
        def last_rows(r, carry):
            start_scatter(last, last % 2, r)
            return carry
        lax.fori_loop(0, tv_ref[last], last_rows, 0)
        wait_gathers(n_real % 2)
        wait_scatters(jnp.maximum(last - 1, 0), n_real % 2, last >= 1)
        wait_scatters(last, last % 2, True)


def _moe_experts(hn_packed, tok_sorted, dst_sorted, tile_expert, tile_valid, tile_base, n_real, w1, b1, w2, b2, layer):
    n_tok, half = hn_packed.shape
    d = 2 * half
    _, n_e, _, two_de = w1.shape
    de = two_de // 2
    tj = _pick(de, (MOE_TJ, 128))
    nj = de // tj
    n_tiles = tile_expert.shape[0]
    assert MOE_TILE % nj == 0 and MOE_TILE % MOE_ROW_SPLIT == 0

    def tile(i, nr):
        return jnp.minimum(i, nr[0] - 1)

    def col(i, j, nr):
        return jnp.where(i < nr[0], j, nj - 1)

    grid_spec = pltpu.PrefetchScalarGridSpec(
        num_scalar_prefetch=6,
        grid=(n_tiles, nj),
        in_specs=[
            pl.BlockSpec(memory_space=pl.ANY),
            pl.BlockSpec((None, None, d, tj), lambda i, j, te, tv, tb, nr, tok, dst: (layer, te[tile(i, nr)], 0, col(i, j, nr))),
            pl.BlockSpec((None, None, d, tj), lambda i, j, te, tv, tb, nr, tok, dst: (layer, te[tile(i, nr)], 0, nj + col(i, j, nr))),
            pl.BlockSpec((None, None, 1, tj), lambda i, j, te, tv, tb, nr, tok, dst: (layer, te[tile(i, nr)], 0, col(i, j, nr))),
            pl.BlockSpec((None, None, 1, tj), lambda i, j, te, tv, tb, nr, tok, dst: (layer, te[tile(i, nr)], 0, nj + col(i, j, nr))),
            pl.BlockSpec((None, None, tj, d), lambda i, j, te, tv, tb, nr, tok, dst: (layer, te[tile(i, nr)], col(i, j, nr), 0)),
            pl.BlockSpec((None, None, 1, d), lambda i, j, te, tv, tb, nr, tok, dst: (layer, te[tile(i, nr)], 0, 0)),
        ],
        out_specs=pl.BlockSpec(memory_space=pl.ANY),
        scratch_shapes=[pltpu.VMEM((2, MOE_TILE, half), U32), pltpu.VMEM((2, MOE_TILE, d), F32),
                        pltpu.VMEM((d, tj), BF16), pltpu.VMEM((d, tj), BF16), pltpu.VMEM((tj, d), BF16),
                        pltpu.SemaphoreType.DMA((2,)), pltpu.SemaphoreType.DMA((2,))],
    )
    part = MOE_TILE // MOE_ROW_SPLIT
    vmem = (2 * MOE_TILE * half * 4 + 2 * MOE_TILE * d * 4 + 2 * 3 * d * tj * 4 + 3 * d * tj * 2
            + part * (d * 2 + 4 * tj * 4 + d * 4))
    return pl.pallas_call(
        functools.partial(_moe_body, n_tiles=n_tiles, nj=nj),
        grid_spec=grid_spec,
        out_shape=jax.ShapeDtypeStruct((TOP_K * n_tok, d), F32),
        compiler_params=pltpu.CompilerParams(dimension_semantics=("arbitrary", "arbitrary"),
                                             vmem_limit_bytes=int(min(VMEM_LIMIT_CAP, vmem + (6 << 20)))),
        name="moe_experts",
    )(tile_expert, tile_valid, tile_base, n_real, tok_sorted, dst_sorted, hn_packed, w1, w1,
      b1.reshape(b1.shape[0], n_e, 1, two_de), b1.reshape(b1.shape[0], n_e, 1, two_de),
      w2, b2.reshape(b2.shape[0], n_e, 1, d))


def _combine_body(x_ref, y_ref, gt_ref, g2_ref, o_ref):
    acc = y_ref[0] * gt_ref[:, 0:1]
    for k in range(1, TOP_K):
        acc = acc + y_ref[k] * gt_ref[:, k:k + 1]
    o_ref[...] = x_ref[...] + g2_ref[...] * acc


def _moe_combine(st, x, y_all, first_row, gates_t, gate2):
    d = x.shape[1]
    tt = st.row_tile(256)
    assert first_row % tt == 0
    first = first_row // tt
    row_of, col_of = (lambda i: i), (lambda i: 0)
    return pl.pallas_call(
        _combine_body,
        grid=(st.rows // tt,),
        in_specs=[pl.BlockSpec((tt, d), lambda i: (i, 0)),
                  pl.BlockSpec((TOP_K, tt, d), lambda i: (0, first + i, 0)),
                  pl.BlockSpec((tt, TOP_K), lambda i: (i, 0)),
                  st.mod_spec(tt, d, row_of, col_of)],
        out_specs=pl.BlockSpec((tt, d), lambda i: (i, 0)),
        out_shape=jax.ShapeDtypeStruct((st.rows, d), F32),
        compiler_params=_params(1, 2 * tt * d * 4 * (3 + TOP_K)),
        name="moe_combine",
    )(x, y_all, gates_t, st.mod_array(gate2))


def _moe_layer(streams, xs, mods, g2, w_router, b_router, w1, b1, w2, b2, layer):
    n_e = w_router.shape[-1]
    w_router_t = w_router[layer].T
    counts = jnp.zeros((n_e, 1), F32)
    hns, idxs, gates, ranks = [], [], [], []
    for st, x, (shift2, scale2, _) in zip(streams, xs, mods):
        hn, idx, gate, rank, counts = _norm_route(st, x, g2, scale2, shift2, w_router_t,
                                                  b_router[layer], counts)
        hns.append(hn), idxs.append(idx), gates.append(gate), ranks.append(rank)
    hn_all = jnp.concatenate(hns, axis=0)
    idx_all = jnp.concatenate(idxs, axis=1)
    rank_all = jnp.concatenate(ranks, axis=1)
    n_tok = hn_all.shape[0]

    n_assign = n_tok * TOP_K
    n_tiles = n_e + n_assign // MOE_TILE
    cnt = counts[:, 0].astype(I32)
    tiles_e = (cnt + MOE_TILE - 1) // MOE_TILE
    tile_end = jnp.cumsum(tiles_e)
    tile_start = tile_end - tiles_e
    n_real = tile_end[-1:]
    expert_start = jnp.cumsum(cnt) - cnt
    tile_ids = jnp.arange(n_tiles, dtype=I32)
    tile_expert = jnp.minimum(jnp.searchsorted(tile_end, tile_ids, side='right'), n_e - 1).astype(I32)
    tile_off = (tile_ids - tile_start[tile_expert]) * MOE_TILE
    tile_valid = jnp.clip(cnt[tile_expert] - tile_off, 0, MOE_TILE).astype(I32)
    tile_base = jnp.clip(expert_start[tile_expert] + tile_off, 0, n_assign - 1).astype(I32)
    pos = expert_start[idx_all] + rank_all
    dst_sorted = jnp.zeros((n_assign + MOE_TILE,), I32).at[pos.reshape(-1)].set(jnp.arange(n_assign, dtype=I32))
    tok_sorted = dst_sorted % n_tok

    y_all = _moe_experts(hn_all, tok_sorted, dst_sorted, tile_expert, tile_valid, tile_base, n_real,
                         w1, b1, w2, b2, layer).reshape(TOP_K, n_tok, -1)

    new_xs, off = [], 0
    for st, x, gate, (_, _, gate2) in zip(streams, xs, gates, mods):
        new_xs.append(_moe_combine(st, x, y_all, off, gate.T, gate2))
        off += st.rows
    return new_xs


def _rope_tables(positions, head_dim):
    inv_freq = ROPE_THETA ** (-jnp.arange(0, head_dim, 2, dtype=F32) / head_dim)
    ang = positions.astype(F32)[:, None] * inv_freq[None, :]
    cos, sin = jnp.cos(ang), jnp.sin(ang)
    return jnp.concatenate([cos, cos], axis=-1), jnp.concatenate([-sin, sin], axis=-1)


def kernel(x_prompt, x_sample, c_prompt, c_sample, cache_k_w128, cache_v_w128, cache_k_w512, cache_v_w512, cache_k_w2048, cache_v_w2048, state_rglru_h, state_conv, w_ada, b_ada, g_norm1, g_norm2, w_qkv, w_attn_o, w_rg_in, conv_w, conv_b, w_gate_a, b_gate_a, w_gate_x, b_gate_x, lru_lambda, w_rg_out, w_router, b_router, w_exp1, b_exp1, w_exp2, b_exp2, g_final):
    caches = (cache_k_w128, cache_v_w128, cache_k_w512, cache_v_w512, cache_k_w2048, cache_v_w2048)
    bp, tp, d = x_prompt.shape
    bs, ts, _ = x_sample.shape
    depth = w_ada.shape[0]
    heads, head_dim = cache_k_w128.shape[3], cache_k_w128.shape[4]
    hd = heads * head_dim
    n_groups = len(DILATED_GROUPS)
    dr = w_rg_out.shape[1]
    assert w_qkv.shape[2] == n_groups * 3 * hd

    prompt = _Stream(bp, tp, per_row=False)
    sample = _Stream(bs, ts, per_row=True)
    streams = (prompt, sample)

    n_c = bp + bs
    n_c_pad = -(-n_c // SUBLANES) * SUBLANES
    c_all = jnp.concatenate([c_prompt, c_sample, jnp.zeros((n_c_pad - n_c, d), F32)], axis=0)
    mod = _ada_mod(c_all, w_ada, b_ada).reshape(depth, n_c_pad, 6, d)
    row_range = ((0, bp), (bp, bp + bs))

    cos_p, sin_p = _rope_tables(jnp.arange(tp, dtype=I32), head_dim)
    cos_s, sin_s = _rope_tables(PAST_LEN + jnp.arange(ts, dtype=I32), head_dim)
    rope = ((cos_p, sin_p), (jnp.tile(cos_s, (bs, 1)), jnp.tile(sin_s, (bs, 1))))

    xs = [x_prompt.reshape(bp * tp, d), x_sample.reshape(bs * ts, d)]
    kv_out = [[[] for _ in range(2 * n_groups)] for _ in streams]
    h_out = [[] for _ in streams]
    conv_out = [[] for _ in streams]
    init_h = (None, state_rglru_h)
    init_conv = (None, state_conv)

    for layer in range(depth):
        j = layer // 2
        mods = []
        for si, st in enumerate(streams):
            lo, hi = row_range[si]
            shift1, scale1, gate1, shift2, scale2, gate2 = [mod[layer, lo:hi, n] for n in range(6)]
            mods.append((shift2, scale2, gate2))
            x = xs[si]
            hn = _norm_mod(st, x, g_norm1[layer], scale1, shift1)
            tm = st.row_tile(1024)
            res_ops = lambda tn: [(x, pl.BlockSpec((tm, tn), lambda jj, ii: (ii, jj))),
                                  (st.mod_array(gate1), st.mod_spec(tm, tn, lambda jj, ii: ii, lambda jj, ii: jj))]
            if layer % 2 == 0:
                cos, sin = rope[si]
                per = max(st.seq // tm, 1)
                tab = (pl.BlockSpec((tm, head_dim), lambda jj, ii: (ii % per, 0)) if not st.per_row
                       else pl.BlockSpec((tm, head_dim), lambda jj, ii: (0, 0)))
                qkv = _matmul("qkv_rope", st.rows,
                              [(hn, pl.BlockSpec((tm, d), lambda jj, ii: (ii, 0)))], _a_cast,
                              w_qkv, j, [(cos, tab), (sin, tab)],
                              functools.partial(_e_rope, head_dim=head_dim),
                              tm, hd, F32, extra_vmem=2 * tm * d * 2)
                q5 = qkv.reshape(st.batch, st.seq, n_groups, 3, heads, head_dim)
                for g, (window, _) in enumerate(DILATED_GROUPS):
                    keep = st.seq if st.per_row else min(window, st.seq)
                    kv_out[si][2 * g].append(q5[:, st.seq - keep:, g, 1])
                    kv_out[si][2 * g + 1].append(q5[:, st.seq - keep:, g, 2])
                tn = _pick(d, (1024, 512, 256, 128))
                if not st.per_row:
                    tma = st.row_tile(512)
                    parts = [_band_attention(qkv, st.batch, st.seq, g, heads, head_dim) for g in range(n_groups)]
                    a_ops = ([(o, pl.BlockSpec((tma, hd), lambda jj, ii: (ii, 0))) for o, _ in parts]
                             + [(l, pl.BlockSpec((tma, LANES), lambda jj, ii: (ii, 0))) for _, l in parts])
                    res = [(x, pl.BlockSpec((tma, tn), lambda jj, ii: (ii, jj))),
                           (st.mod_array(gate1), st.mod_spec(tma, tn, lambda jj, ii: ii, lambda jj, ii: jj))]
                    xs[si] = _matmul("attn_out", st.rows, a_ops,
                                     functools.partial(_a_merge, heads=heads, head_dim=head_dim),
                                     w_attn_o, j, res, _e_residual, tma, tn, F32,
                                     extra_vmem=2 * tma * (3 * hd + 3 * LANES + tn) * 4)
                else:
                    comb = _decode_attention(qkv, caches, j, st.batch, st.seq, heads, head_dim)
                    xs[si] = _matmul("attn_out", st.rows,
                                     [(comb, pl.BlockSpec((tm, hd), lambda jj, ii: (ii, 0)))], _a_cast,
                                     w_attn_o, j, res_ops(tn), _e_residual, tm, tn, F32,
                                     extra_vmem=2 * tm * (hd + tn) * 4)
            else:
                tn = _pick(2 * dr, (1024, 512, 256, 128))
                u = _matmul("rg_in", st.rows,
                            [(hn, pl.BlockSpec((tm, d), lambda jj, ii: (ii, 0)))], _a_cast,
                            w_rg_in, j, [], _e_store, tm, tn, F32, extra_vmem=2 * tm * d * 2)
                h0 = jnp.zeros((st.batch, dr), F32) if init_h[si] is None else init_h[si][j]
                cb0 = jnp.zeros((st.batch, CONV_W - 1, dr), F32) if init_conv[si] is None else init_conv[si][j]
                y, h_last, c_new = _rg_core(st, u, cb0, h0, conv_w[j], conv_b[j], w_gate_a[j], b_gate_a[j],
                                            w_gate_x[j], b_gate_x[j], lru_lambda[j],
                                            F32 if st.per_row else BF16)
                h_out[si].append(h_last)
                conv_out[si].append(c_new)
                tn = _pick(d, (1024, 512, 256, 128))
                xs[si] = _matmul("rg_out", st.rows,
                                 [(y, pl.BlockSpec((tm, dr), lambda jj, ii: (ii, 0)))], _a_cast,
                                 w_rg_out, j, res_ops(tn), _e_residual, tm, tn, F32,
                                 extra_vmem=2 * tm * (dr + tn) * 4)
        xs = _moe_layer(streams, xs, mods, g_norm2[layer], w_router, b_router,
                        w_exp1, b_exp1, w_exp2, b_exp2, layer)

    ys = [_final_norm(st, x, g_final).reshape(st.batch, st.seq, d) for st, x in zip(streams, xs)]
    outs = []
    for si in range(2):
        outs.append([jnp.stack(a, axis=0) for a in kv_out[si]]
                    + [jnp.stack(h_out[si], axis=0), jnp.stack(conv_out[si], axis=0)])
    return (ys[0], ys[1], *outs[0], *outs[1])
```

```python
import functools

import numpy as np
import jax
import jax.numpy as jnp
from jax import lax
from jax.experimental import pallas as pl
from jax.experimental.pallas import tpu as pltpu

F32 = jnp.float32
BF16 = jnp.bfloat16
I32 = jnp.int32

DILATED_GROUPS = ((128, 1), (512, 4), (2048, 16))
ROPE_THETA = 10000.0
PAST_LEN = 16384
CONV_W = 4
LRU_C = 8.0
TOP_K = 4
SWIGLU_LIMIT = 7.0
SWIGLU_ALPHA = 1.702
NORM_EPS = 1e-6

V7X_VMEM_BYTES = 64 * 1024 * 1024
VMEM_LIMIT_CAP = 56 * 1024 * 1024
LANES = 128
SUBLANES = 8

MOE_TILE = 1152
MOE_TJ = 256
MOE_ROW_SPLIT = 4
U32 = jnp.uint32


def _pack_halves(x):
    half = x.shape[1] // 2
    lo = lax.bitcast_convert_type(x[:, :half].astype(BF16).astype(F32), U32)
    hi = lax.bitcast_convert_type(x[:, half:].astype(BF16).astype(F32), U32)
    return (lo >> 16) | hi


def _unpack_halves(w):
    lo = lax.bitcast_convert_type(w << 16, F32).astype(BF16)
    hi = lax.bitcast_convert_type(w & U32(0xFFFF0000), F32).astype(BF16)
    return lo, hi


def _params(n_axes, vmem_bytes):
    limit = int(min(VMEM_LIMIT_CAP, max(vmem_bytes * 5 // 4 + (4 << 20), 16 << 20)))
    return pltpu.CompilerParams(dimension_semantics=("arbitrary",) * n_axes,
                                vmem_limit_bytes=limit)


def _pick(n, candidates):
    for c in candidates:
        if n % c == 0:
            return c
    return n


class _Stream:
    def __init__(self, batch, seq, per_row):
        self.batch, self.seq, self.per_row = batch, seq, per_row
        self.rows = batch * seq

    def mod_array(self, vec):
        if self.per_row:
            return jnp.repeat(vec, self.seq, axis=0)
        return vec[:, None, :]

    def mod_spec(self, tm, tn, row_of, col_of):
        if self.per_row:
            return pl.BlockSpec((tm, tn), lambda *g: (row_of(*g), col_of(*g)))
        per_batch = self.seq // tm
        return pl.BlockSpec((None, 1, tn), lambda *g: (row_of(*g) // per_batch, 0, col_of(*g)))

    def row_tile(self, cap):
        if self.per_row:
            return self.rows
        return _pick(self.seq, [c for c in (1024, 512, 256, 128, 64, 32, 16, 8) if c <= cap])


def _ada_body(c_ref, w_ref, b_ref, o_ref):
    c = c_ref[...]
    s = (c * jax.nn.sigmoid(c)).astype(BF16)
    o_ref[0] = jnp.dot(s, w_ref[0].astype(BF16), preferred_element_type=F32) + b_ref[0]


def _ada_mod(c_all, w_ada, b_ada):
    n_layers, d, n6 = w_ada.shape
    mc = c_all.shape[0]
    tn = _pick(n6, (1024, 512, 256, 128))
    vmem = 2 * (mc * d * 4 + d * tn * 4 + tn * 4 + mc * tn * 4) + d * tn * 2
    return pl.pallas_call(
        _ada_body,
        grid=(n_layers, n6 // tn),
        in_specs=[pl.BlockSpec((mc, d), lambda l, j: (0, 0)),
                  pl.BlockSpec((1, d, tn), lambda l, j: (l, 0, j)),
                  pl.BlockSpec((1, 1, tn), lambda l, j: (l, 0, j))],
        out_specs=pl.BlockSpec((1, mc, tn), lambda l, j: (l, 0, j)),
        out_shape=jax.ShapeDtypeStruct((n_layers, mc, n6), F32),
        compiler_params=_params(2, vmem),
        name="ada_mod",
    )(c_all, w_ada, b_ada.reshape(n_layers, 1, n6))


def _rms(x):
    return x * lax.rsqrt(jnp.mean(x * x, axis=-1, keepdims=True) + NORM_EPS)


def _norm_body(x_ref, g_ref, o_ref):
    o_ref[...] = (_rms(x_ref[...]) * g_ref[...]).astype(o_ref.dtype)


def _norm_mod_body(x_ref, g_ref, sc_ref, sh_ref, o_ref):
    hn = (_rms(x_ref[...]) * g_ref[...]) * (1.0 + sc_ref[...]) + sh_ref[...]
    o_ref[...] = hn.astype(o_ref.dtype)


def _final_norm(st, x, g):
    d = x.shape[1]
    tt = st.row_tile(512)
    return pl.pallas_call(
        _norm_body,
        grid=(st.rows // tt,),
        in_specs=[pl.BlockSpec((tt, d), lambda i: (i, 0)),
                  pl.BlockSpec((1, d), lambda i: (0, 0))],
        out_specs=pl.BlockSpec((tt, d), lambda i: (i, 0)),
        out_shape=jax.ShapeDtypeStruct((st.rows, d), F32),
        compiler_params=_params(1, 4 * tt * d * 4),
        name="final_norm",
    )(x, g.reshape(1, d))


def _norm_mod(st, x, g, scale, shift):
    d = x.shape[1]
    tt = st.row_tile(512)
    row_of, col_of = (lambda i: i), (lambda i: 0)
    return pl.pallas_call(
        _norm_mod_body,
        grid=(st.rows // tt,),
        in_specs=[pl.BlockSpec((tt, d), lambda i: (i, 0)),
                  pl.BlockSpec((1, d), lambda i: (0, 0)),
                  st.mod_spec(tt, d, row_of, col_of),
                  st.mod_spec(tt, d, row_of, col_of)],
        out_specs=pl.BlockSpec((tt, d), lambda i: (i, 0)),
        out_shape=jax.ShapeDtypeStruct((st.rows, d), BF16),
        compiler_params=_params(1, 2 * tt * d * (4 + 2) + 6 * tt * d * 4 * int(st.per_row)),
        name="norm_mod",
    )(x, g.reshape(1, d), st.mod_array(scale), st.mod_array(shift))


def _norm_route_body(x_ref, g_ref, sc_ref, sh_ref, wr_ref, br_ref, cin_ref,
                     hn_ref, idx_ref, gate_ref, rank_ref, cnt_ref, carry_ref):
    @pl.when(pl.program_id(0) == 0)
    def _():
        carry_ref[...] = cin_ref[...]

    hn = (_rms(x_ref[...]) * g_ref[...]) * (1.0 + sc_ref[...]) + sh_ref[...]
    packed = _pack_halves(hn)
    groups = packed.shape[1] // LANES
    for grp in range(groups):
        hn_ref[pl.ds(grp, packed.shape[0], stride=groups), :] = packed[:, grp * LANES:(grp + 1) * LANES]
    logits = lax.dot_general(wr_ref[...], hn, (((1,), (1,)), ((), ())),
                             precision=lax.Precision.HIGHEST,
                             preferred_element_type=F32) + br_ref[...]
    n_e, tt = logits.shape
    e_iota = lax.broadcasted_iota(I32, (n_e, tt), 0)
    vals, sels = [], []
    cur = logits
    for k in range(TOP_K):
        m = jnp.max(cur, axis=0, keepdims=True)
        idx = jnp.min(jnp.where(cur == m, e_iota, n_e), axis=0, keepdims=True)
        sel = e_iota == idx
        idx_ref[k:k + 1, :] = idx
        vals.append(m)
        sels.append(sel)
        cur = jnp.where(sel, -jnp.inf, cur)
    exps = [jnp.exp(v - vals[0]) for v in vals]
    den = exps[0]
    for e in exps[1:]:
        den = den + e
    upper = jnp.where(lax.broadcasted_iota(I32, (tt, tt), 0) <= lax.broadcasted_iota(I32, (tt, tt), 1),
                      1.0, 0.0).astype(BF16)
    base = carry_ref[...]
    for k in range(TOP_K):
        gate_ref[k:k + 1, :] = exps[k] / den
        onehot = jnp.where(sels[k], 1.0, 0.0)
        cum = jnp.dot(onehot.astype(BF16), upper, preferred_element_type=F32)
        rank = jnp.sum(onehot * (cum - 1.0 + base), axis=0, keepdims=True)
        rank_ref[k:k + 1, :] = rank.astype(I32)
        base = base + jnp.sum(onehot, axis=1, keepdims=True)
    carry_ref[...] = base
    cnt_ref[...] = base


def _norm_route(st, x, g, scale, shift, w_router_t, b_router, count_in):
    d = x.shape[1]
    n_e = w_router_t.shape[0]
    tt = st.row_tile(512)
    groups = d // 2 // LANES
    row_of, col_of = (lambda i: i), (lambda i: 0)
    tok_spec = pl.BlockSpec((TOP_K, tt), lambda i: (0, i))
    vmem = 2 * tt * d * (4 + 2) + 6 * tt * d * 4 * int(st.per_row) + tt * d * 8 + tt * tt * 8
    return pl.pallas_call(
        _norm_route_body,
        grid=(st.rows // tt,),
        in_specs=[pl.BlockSpec((tt, d), lambda i: (i, 0)),
                  pl.BlockSpec((1, d), lambda i: (0, 0)),
                  st.mod_spec(tt, d, row_of, col_of),
                  st.mod_spec(tt, d, row_of, col_of),
                  pl.BlockSpec((n_e, d), lambda i: (0, 0)),
                  pl.BlockSpec((n_e, 1), lambda i: (0, 0)),
                  pl.BlockSpec((n_e, 1), lambda i: (0, 0))],
        out_specs=[pl.BlockSpec((tt * groups, LANES), lambda i: (i, 0)), tok_spec, tok_spec, tok_spec,
                   pl.BlockSpec((n_e, 1), lambda i: (0, 0))],
        out_shape=[jax.ShapeDtypeStruct((st.rows * groups, LANES), U32),
                   jax.ShapeDtypeStruct((TOP_K, st.rows), I32),
                   jax.ShapeDtypeStruct((TOP_K, st.rows), F32),
                   jax.ShapeDtypeStruct((TOP_K, st.rows), I32),
                   jax.ShapeDtypeStruct((n_e, 1), F32)],
        scratch_shapes=[pltpu.VMEM((n_e, 1), F32)],
        compiler_params=_params(1, vmem),
        name="norm_route",
    )(x, g.reshape(1, d), st.mod_array(scale), st.mod_array(shift), w_router_t,
      b_router.reshape(n_e, 1), count_in)


def _mm_body(*refs, n_a, a_fn, n_e, e_fn):
    a_refs = refs[:n_a]
    w_ref = refs[n_a]
    e_refs = refs[n_a + 1:n_a + 1 + n_e]
    o_ref = refs[n_a + 1 + n_e]
    wbf_ref = refs[n_a + 2 + n_e]

    @pl.when(pl.program_id(1) == 0)
    def _():
        wbf_ref[...] = w_ref[...].astype(BF16)

    acc = jnp.dot(a_fn(*a_refs), wbf_ref[...], preferred_element_type=F32)
    e_fn(acc, o_ref, *e_refs)


def _matmul(name, rows, a_ops, a_fn, w_stack, layer, e_ops, e_fn, tm, tn, out_dtype, extra_vmem=0):
    _, k, n = w_stack.shape
    in_specs = ([s for _, s in a_ops]
                + [pl.BlockSpec((None, k, tn), lambda j, i: (layer, 0, j))]
                + [s for _, s in e_ops])
    vmem = 2 * k * tn * 4 + k * tn * 2 + 2 * tm * tn * 4 + extra_vmem
    return pl.pallas_call(
        functools.partial(_mm_body, n_a=len(a_ops), a_fn=a_fn, n_e=len(e_ops), e_fn=e_fn),
        grid=(n // tn, rows // tm),
        in_specs=in_specs,
        out_specs=pl.BlockSpec((tm, tn), lambda j, i: (i, j)),
        out_shape=jax.ShapeDtypeStruct((rows, n), out_dtype),
        scratch_shapes=[pltpu.VMEM((k, tn), BF16)],
        compiler_params=_params(2, vmem),
        name=name,
    )(*[a for a, _ in a_ops], w_stack, *[e for e, _ in e_ops])


def _a_cast(a_ref):
    return a_ref[...].astype(BF16)


def _e_store(acc, o_ref):
    o_ref[...] = acc.astype(o_ref.dtype)


def _e_residual(acc, o_ref, x_ref, gate_ref):
    o_ref[...] = x_ref[...] + gate_ref[...] * acc


def _e_rope(acc, o_ref, cos_ref, sin_ref, *, head_dim):
    is_v = pl.program_id(0) % 3 == 2

    @pl.when(is_v)
    def _():
        o_ref[...] = acc

    @pl.when(jnp.logical_not(is_v))
    def _():
        cos = cos_ref[...]
        sin = sin_ref[...]
        for h in range(acc.shape[1] // head_dim):
            t = acc[:, h * head_dim:(h + 1) * head_dim]
            o_ref[:, h * head_dim:(h + 1) * head_dim] = t * cos + pltpu.roll(t, head_dim // 2, 1) * sin


def _a_merge(o0_ref, o1_ref, o2_ref, l0_ref, l1_ref, l2_ref, *, heads, head_dim):
    lses = [l0_ref[...], l1_ref[...], l2_ref[...]]
    outs = [o0_ref, o1_ref, o2_ref]
    mx = jnp.maximum(jnp.maximum(lses[0], lses[1]), lses[2])
    es = [jnp.exp(l - mx) for l in lses]
    den = es[0] + es[1] + es[2]
    ws = [e / den for e in es]
    cols = []
    for h in range(heads):
        sl = slice(h * head_dim, (h + 1) * head_dim)
        c = ws[0][:, h:h + 1] * outs[0][:, sl]
        c = c + ws[1][:, h:h + 1] * outs[1][:, sl]
        c = c + ws[2][:, h:h + 1] * outs[2][:, sl]
        cols.append(c.astype(BF16))
    return jnp.concatenate(cols, axis=1)


def _attend(q, k_cat, v_cat, prev_bias, scale):
    nk = q.shape[0]
    s = lax.dot_general(q, k_cat, (((1,), (1,)), ((), ())), preferred_element_type=F32) * scale
    qi = lax.broadcasted_iota(I32, (nk, 2 * nk), 0)
    kj = lax.broadcasted_iota(I32, (nk, 2 * nk), 1)
    s = s + jnp.where(kj < nk, prev_bias, 0.0)
    s = jnp.where(jnp.logical_and(kj >= qi, kj <= qi + nk), s, -jnp.inf)
    m = jnp.max(s, axis=-1, keepdims=True)
    p = jnp.exp(s - m)
    l = jnp.sum(p, axis=-1, keepdims=True)
    o = jnp.dot((p / l).astype(BF16), v_cat, preferred_element_type=F32)
    return o, m + jnp.log(l)


def _band_dense_body(q_ref, kp_ref, kc_ref, vp_ref, vc_ref, o_ref, lse_ref, *, heads, head_dim, nk):
    scale = head_dim ** -0.5
    first_bias = jnp.where(pl.program_id(1) > 0, 0.0, -jnp.inf)
    lane = lax.broadcasted_iota(I32, (nk, LANES), 1)
    for s in range(q_ref.shape[0] // nk):
        rows = slice(s * nk, (s + 1) * nk)
        lse_all = jnp.zeros((nk, LANES), F32)
        for h in range(heads):
            sl = slice(h * head_dim, (h + 1) * head_dim)
            q = q_ref[rows, sl].astype(BF16)
            if s == 0:
                k_cat = jnp.concatenate([kp_ref[:, sl], kc_ref[rows, sl]], axis=0).astype(BF16)
                v_cat = jnp.concatenate([vp_ref[:, sl], vc_ref[rows, sl]], axis=0).astype(BF16)
                bias = first_bias
            else:
                k_cat = kc_ref[(s - 1) * nk:(s + 1) * nk, sl].astype(BF16)
                v_cat = vc_ref[(s - 1) * nk:(s + 1) * nk, sl].astype(BF16)
                bias = 0.0
            o, lse = _attend(q, k_cat, v_cat, bias, scale)
            o_ref[rows, sl] = o
            lse_all = jnp.where(lane == h, lse, lse_all)
        lse_ref[rows, :] = lse_all


def _band_span_body(q_ref, kp_ref, kc_ref, vp_ref, vc_ref, o_ref, lse_ref, *, dil, heads_per_step, head_dim):
    scale = head_dim ** -0.5
    nk = q_ref.shape[0] // dil
    head0 = pl.program_id(2) * heads_per_step
    prev_bias = jnp.where(pl.program_id(1) > 0, 0.0, -jnp.inf)
    lane = lax.broadcasted_iota(I32, (nk, LANES), 1)

    @pl.when(pl.program_id(2) == 0)
    def _():
        lse_ref[...] = jnp.zeros(lse_ref.shape, F32)

    for r in range(dil):
        rows = pl.ds(r, nk, stride=dil)
        lse_all = lse_ref[rows, :]
        for h in range(heads_per_step):
            sl = slice(h * head_dim, (h + 1) * head_dim)
            q = q_ref[rows, sl].astype(BF16)
            k_cat = jnp.concatenate([kp_ref[rows, sl], kc_ref[rows, sl]], axis=0).astype(BF16)
            v_cat = jnp.concatenate([vp_ref[rows, sl], vc_ref[rows, sl]], axis=0).astype(BF16)
            o, lse = _attend(q, k_cat, v_cat, prev_bias, scale)
            o_ref[rows, sl] = o
            lse_all = jnp.where(lane == head0 + h, lse, lse_all)
        lse_ref[rows, :] = lse_all


def _band_attention(qkv, batch, seq, g, heads, head_dim):
    window, dil = DILATED_GROUPS[g]
    nk = window // dil
    hd = heads * head_dim
    rows = batch * seq
    assert seq % window == 0, "prompt length must be a multiple of every window span"
    out_shape = [jax.ShapeDtypeStruct((rows, hd), F32), jax.ShapeDtypeStruct((rows, LANES), F32)]
    if dil == 1:
        tq = _pick(seq, (4 * nk, 2 * nk, nk))
        per, sub = seq // tq, tq // nk
        cur = lambda which: pl.BlockSpec((tq, hd), lambda b, n: (b * per + n, g * 3 + which))
        prev = lambda which: pl.BlockSpec(
            (nk, hd), lambda b, n: (b * per * sub + jnp.maximum(n * sub - 1, 0), g * 3 + which))
        return pl.pallas_call(
            functools.partial(_band_dense_body, heads=heads, head_dim=head_dim, nk=nk),
            grid=(batch, per),
            in_specs=[cur(0), prev(1), cur(1), prev(2), cur(2)],
            out_specs=[pl.BlockSpec((tq, hd), lambda b, n: (b * per + n, 0)),
                       pl.BlockSpec((tq, LANES), lambda b, n: (b * per + n, 0))],
            out_shape=out_shape,
            compiler_params=_params(2, 2 * (4 * tq + 2 * nk) * hd * 4 + (8 << 20)),
            name=f"band_attention_g{g}",
        )(qkv, qkv, qkv, qkv, qkv)
    assert head_dim == LANES
    hps = 1
    n_hs = heads // hps
    per = seq // window
    wide = hps * head_dim

    def spec(which, back):
        return pl.BlockSpec((window, wide),
                            lambda b, n, hs: (b * per + jnp.maximum(n - back, 0), (g * 3 + which) * n_hs + hs))

    return pl.pallas_call(
        functools.partial(_band_span_body, dil=dil, heads_per_step=hps, head_dim=head_dim),
        grid=(batch, per, n_hs),
        in_specs=[spec(0, 0), spec(1, 1), spec(1, 0), spec(2, 1), spec(2, 0)],
        out_specs=[pl.BlockSpec((window, wide), lambda b, n, hs: (b * per + n, hs)),
                   pl.BlockSpec((window, LANES), lambda b, n, hs: (b * per + n, 0))],
        out_shape=out_shape,
        compiler_params=_params(3, 2 * (6 * window * wide + window * LANES) * 4 + (8 << 20)),
        name=f"band_attention_g{g}",
    )(qkv, qkv, qkv, qkv, qkv)


def _decode_body(qkv_ref, *refs, heads, head_dim, seq):
    cache_refs = refs[:-1]
    o_ref = refs[-1]
    hd = heads * head_dim
    scale = head_dim ** -0.5
    nt = (((1,), (1,)), ((), ()))
    diag = (lax.broadcasted_iota(I32, (heads, hd), 0)
            == lax.broadcasted_iota(I32, (heads, hd), 1) // head_dim)
    for t in range(seq):
        outs, lses = [], []
        for g, (window, dil) in enumerate(DILATED_GROUPS):
            nk = window // dil
            base = g * 3 * hd
            q = jnp.broadcast_to(qkv_ref[t:t + 1, base:base + hd], (heads, hd))
            qbd = jnp.where(diag, q, 0.0).astype(BF16)
            k_new = qkv_ref[:, base + hd:base + 2 * hd].astype(BF16)
            v_new = qkv_ref[:, base + 2 * hd:base + 3 * hd].astype(BF16)
            res, first = t % dil, t // dil
            k_old = cache_refs[2 * g][:, res * hd:(res + 1) * hd].astype(BF16)
            v_old = cache_refs[2 * g + 1][:, res * hd:(res + 1) * hd].astype(BF16)
            s_old = lax.dot_general(qbd, k_old, nt, preferred_element_type=F32) * scale
            s_new = lax.dot_general(qbd, k_new, nt, preferred_element_type=F32) * scale
            mi = lax.broadcasted_iota(I32, (heads, nk), 1)
            ti = lax.broadcasted_iota(I32, (heads, seq), 1)
            s_old = jnp.where(mi >= first, s_old, -jnp.inf)
            s_new = jnp.where(jnp.logical_and(ti <= t, ((t - ti) & (dil - 1)) == 0), s_new, -jnp.inf)
            m = jnp.maximum(jnp.max(s_old, axis=-1, keepdims=True), jnp.max(s_new, axis=-1, keepdims=True))
            p_old = jnp.exp(s_old - m)
            p_new = jnp.exp(s_new - m)
            l = jnp.sum(p_old, axis=-1, keepdims=True) + jnp.sum(p_new, axis=-1, keepdims=True)
            o = (jnp.dot((p_old / l).astype(BF16), v_old, preferred_element_type=F32)
                 + jnp.dot((p_new / l).astype(BF16), v_new, preferred_element_type=F32))
            outs.append(o)
            lses.append(m + jnp.log(l))
        mx = jnp.maximum(jnp.maximum(lses[0], lses[1]), lses[2])
        es = [jnp.exp(l - mx) for l in lses]
        den = es[0] + es[1] + es[2]
        comb = (es[0] / den) * outs[0] + (es[1] / den) * outs[1] + (es[2] / den) * outs[2]
        o_ref[t:t + 1, :] = jnp.sum(jnp.where(diag, comb, 0.0), axis=0, keepdims=True)


def _decode_attention(qkv, caches, layer, batch, seq, heads, head_dim):
    hd = heads * head_dim
    ops, specs = [], []
    vmem = seq * qkv.shape[1] * 4
    for g, (window, dil) in enumerate(DILATED_GROUPS):
        assert dil & (dil - 1) == 0 and seq <= window
        nk = window // dil
        for c in caches[2 * g:2 * g + 2]:
            assert c.shape[2] == window, "the cache must hold exactly one window of past positions"
            n_res = min(dil, seq)
            ops.append(c.reshape(c.shape[0], c.shape[1], nk, dil * hd))
            specs.append(pl.BlockSpec((None, None, nk, n_res * hd), lambda b: (layer, b, 0, 0)))
            vmem += nk * n_res * hd * 4
    return pl.pallas_call(
        functools.partial(_decode_body, heads=heads, head_dim=head_dim, seq=seq),
        grid=(batch,),
        in_specs=[pl.BlockSpec((seq, qkv.shape[1]), lambda b: (b, 0))] + specs,
        out_specs=pl.BlockSpec((seq, hd), lambda b: (b, 0)),
        out_shape=jax.ShapeDtypeStruct((batch * seq, hd), F32),
        compiler_params=_params(1, 2 * vmem),
        name="decode_attention",
    )(qkv, *ops)


def _softplus(z):
    return jnp.maximum(z, 0.0) + jnp.log1p(jnp.exp(-jnp.abs(z)))


def _rg_body(gate_ref, xr_ref, cbuf_ref, cw_ref, cb_ref, wga_ref, bga_ref, wgx_ref, bgx_ref,
             lam_ref, h0_ref, y_ref, hlast_ref, cnew_ref, xs_ref, a_ref, b_ref, h_ref,
             *, tt, n_blocks):
    ti = pl.program_id(1)
    halo = CONV_W - 1
    lo = SUBLANES - halo

    @pl.when(ti == 0)
    def _():
        xs_ref[lo:SUBLANES, :] = cbuf_ref[...]
        h_ref[...] = h0_ref[...]

    @pl.when(ti > 0)
    def _():
        xs_ref[lo:SUBLANES, :] = xs_ref[tt + lo:tt + SUBLANES, :]

    xs_ref[SUBLANES:SUBLANES + tt, :] = xr_ref[...]
    conv = cw_ref[0:1, :] * xs_ref[lo:lo + tt, :]
    for tap in range(1, CONV_W):
        conv = conv + cw_ref[tap:tap + 1, :] * xs_ref[lo + tap:lo + tap + tt, :]
    xc = cb_ref[...] + conv

    @pl.when(ti == pl.num_programs(1) - 1)
    def _():
        cnew_ref[...] = xs_ref[tt + lo:tt + SUBLANES, :]

    xcb = xc.astype(BF16)
    bw = xc.shape[1] // n_blocks
    for n in range(n_blocks):
        sl = slice(n * bw, (n + 1) * bw)
        ga = jnp.dot(xcb[:, sl], wga_ref[n].astype(BF16), preferred_element_type=F32) + bga_ref[:, sl]
        gx = jnp.dot(xcb[:, sl], wgx_ref[n].astype(BF16), preferred_element_type=F32) + bgx_ref[:, sl]
        r = jax.nn.sigmoid(ga)
        i = jax.nn.sigmoid(gx)
        log_a = -LRU_C * r * _softplus(-lam_ref[:, sl])
        a = jnp.exp(log_a)
        a_ref[:, sl] = a
        b_ref[:, sl] = jnp.sqrt(-jnp.tanh(log_a) * (a * a + 1.0)) * (i * xc[:, sl])

    def step(t, h):
        h = a_ref[pl.ds(t, 1), :] * h + b_ref[pl.ds(t, 1), :]
        b_ref[pl.ds(t, 1), :] = h
        return h

    h = lax.fori_loop(0, tt, step, h_ref[...])
    h_ref[...] = h

    @pl.when(ti == pl.num_programs(1) - 1)
    def _():
        hlast_ref[...] = h

    y_ref[...] = (b_ref[...] * jax.nn.gelu(gate_ref[...])).astype(y_ref.dtype)


def _rg_core(st, u, conv_buf, h0, conv_w, conv_b, w_ga, b_ga, w_gx, b_gx, lam, out_dtype):
    dr = u.shape[1] // 2
    n_blocks, bw, _ = w_ga.shape
    batch, seq = st.batch, st.seq
    tt = _pick(seq, (256, 128, 64, 32, 16, 8))
    per = seq // tt
    halo = CONV_W - 1
    vec = lambda: pl.BlockSpec((1, dr), lambda b, t: (0, 0))
    gate_w = lambda: pl.BlockSpec((n_blocks, bw, bw), lambda b, t: (0, 0, 0))
    vmem = 2 * (2 * tt * dr * 4 + 2 * n_blocks * bw * bw * 4 + tt * dr * 4) + 3 * (tt + 8) * dr * 4 + 6 * tt * dr * 4
    y, h_last, c_new = pl.pallas_call(
        functools.partial(_rg_body, tt=tt, n_blocks=n_blocks),
        grid=(batch, per),
        in_specs=[pl.BlockSpec((tt, dr), lambda b, t: (b * per + t, 0)),
                  pl.BlockSpec((tt, dr), lambda b, t: (b * per + t, 1)),
                  pl.BlockSpec((None, halo, dr), lambda b, t: (b, 0, 0)),
                  pl.BlockSpec((CONV_W, dr), lambda b, t: (0, 0)),
                  vec(), gate_w(), vec(), gate_w(), vec(), vec(),
                  pl.BlockSpec((None, 1, dr), lambda b, t: (b, 0, 0))],
        out_specs=[pl.BlockSpec((tt, dr), lambda b, t: (b * per + t, 0)),
                   pl.BlockSpec((None, 1, dr), lambda b, t: (b, 0, 0)),
                   pl.BlockSpec((None, halo, dr), lambda b, t: (b, 0, 0))],
        out_shape=[jax.ShapeDtypeStruct((st.rows, dr), out_dtype),
                   jax.ShapeDtypeStruct((batch, 1, dr), F32),
                   jax.ShapeDtypeStruct((batch, halo, dr), F32)],
        scratch_shapes=[pltpu.VMEM((tt + SUBLANES, dr), F32), pltpu.VMEM((tt, dr), F32),
                        pltpu.VMEM((tt, dr), F32), pltpu.VMEM((1, dr), F32)],
        compiler_params=_params(2, vmem),
        name="rg_core",
    )(u, u, conv_buf, conv_w, conv_b.reshape(1, dr), w_ga, b_ga.reshape(1, dr), w_gx,
      b_gx.reshape(1, dr), lam.reshape(1, dr), h0.reshape(batch, 1, dr))
    return y, h_last.reshape(batch, dr), c_new


def _moe_body(te_ref, tv_ref, tb_ref, nr_ref, tok_ref, dst_ref,
              hn_hbm, w1g_ref, w1l_ref, b1g_ref, b1l_ref, w2_ref, b2_ref, y_hbm,
              xbuf, acc, ystage, wg_s, wl_s, w2_s, gsem, ssem, *, n_tiles, nj):
    i = pl.program_id(0)
    j = pl.program_id(1)
    n_real = nr_ref[0]
    slot = i % 2
    other = 1 - slot
    last = n_real - 1
    tile, d = acc.shape
    gx = xbuf.shape[1] // tile
    gy = d // LANES
    per_gather = tile // nj
    per_scatter = -(-tile // (nj - 1))

    def start_gather(t, s, r):
        tok = pl.multiple_of(tok_ref[tb_ref[t] + r], gx)
        pltpu.make_async_copy(hn_hbm.at[pl.ds(tok, gx)], xbuf.at[s, pl.ds(pl.multiple_of(r * gx, gx), gx)],
                              gsem.at[s]).start()

    def wait_gathers(s):
        pltpu.make_async_copy(hn_hbm.at[pl.ds(0, tile * gx)], xbuf.at[s], gsem.at[s]).wait()

    def start_scatter(t, r):
        dst = pl.multiple_of(dst_ref[tb_ref[t] + r], gy)
        pltpu.make_async_copy(ystage.at[pl.ds(pl.multiple_of(r * gy, gy), gy)], y_hbm.at[pl.ds(dst, gy)],
                              ssem).start()

    def wait_scatters(t, enabled):
        n = tv_ref[t]
        rows = 1
        while rows <= tile:
            @pl.when(jnp.logical_and(enabled, (n & rows) != 0))
            def _():
                pltpu.make_async_copy(ystage.at[pl.ds(0, rows * gy)], y_hbm.at[pl.ds(0, rows * gy)], ssem).wait()
            rows *= 2

    def column_step(final):
        nxt = jnp.minimum(i + 1, last)
        prev = jnp.maximum(i - 1, 0)
        for r in range(per_gather):
            start_gather(nxt, other, j * per_gather + r)
        if not final:
            for r in range(per_scatter):
                rr = j * per_scatter + r

                @pl.when(jnp.logical_and(i >= 1, rr < tv_ref[prev]))
                def _():
                    start_scatter(prev, rr)

        wg_s[...] = w1g_ref[...].astype(BF16)
        wl_s[...] = w1l_ref[...].astype(BF16)
        w2_s[...] = w2_ref[...].astype(BF16)
        part = tile // MOE_ROW_SPLIT
        for p in range(MOE_ROW_SPLIT):
            rows = slice(p * part, (p + 1) * part)
            halves = [_unpack_halves(xbuf[slot, pl.ds(p * part * gx + grp, part, stride=gx), :])
                      for grp in range(gx)]
            x = jnp.concatenate([lo for lo, _ in halves] + [hi for _, hi in halves], axis=1)
            glu = jnp.dot(x, wg_s[...], preferred_element_type=F32) + b1g_ref[...]
            lin = jnp.dot(x, wl_s[...], preferred_element_type=F32) + b1l_ref[...]
            glu = jnp.minimum(glu, SWIGLU_LIMIT)
            lin = jnp.clip(lin, -SWIGLU_LIMIT, SWIGLU_LIMIT)
            act = glu * jax.nn.sigmoid(SWIGLU_ALPHA * glu) * (lin + 1.0)
            res = acc[rows, :] + jnp.dot(act.astype(BF16), w2_s[...], preferred_element_type=F32)
            if final:
                for grp in range(gy):
                    ystage[pl.ds(p * part * gy + grp, part, stride=gy), :] = res[:, grp * LANES:(grp + 1) * LANES]
            else:
                acc[rows, :] = res

    @pl.when(i < n_real)
    def _():
        @pl.when(j == 0)
        def _():
            @pl.when(i == 0)
            def _():
                def first_rows(r, carry):
                    start_gather(0, 0, r)
                    return carry
                lax.fori_loop(0, tile, first_rows, 0)

            wait_gathers(slot)
            acc[...] = jnp.broadcast_to(b2_ref[...], acc.shape)

        @pl.when(j < nj - 1)
        def _():
            column_step(False)

        @pl.when(j == nj - 1)
        def _():
            wait_scatters(jnp.maximum(i - 1, 0), i >= 1)
            column_step(True)

    @pl.when(jnp.logical_and(i == n_tiles - 1, j == nj - 1))
    def _():
        def last_rows(r, carry):
            start_scatter(last, r)
            return carry
        lax.fori_loop(0, tv_ref[last], last_rows, 0)
        wait_gathers(n_real % 2)
        wait_scatters(last, True)


def _moe_experts(hn_packed, tok_sorted, dst_sorted, tile_expert, tile_valid, tile_base, n_real, w1, b1, w2, b2, layer):
    _, n_e, d, two_de = w1.shape
    half = d // 2
    gx, gy = half // LANES, d // LANES
    n_tok = hn_packed.shape[0] // gx
    de = two_de // 2
    tj = _pick(de, (MOE_TJ, 128))
    nj = de // tj
    n_tiles = tile_expert.shape[0]
    assert nj >= 2 and MOE_TILE % nj == 0 and MOE_TILE % MOE_ROW_SPLIT == 0 and n_tok >= MOE_TILE

    def tile(i, nr):
        return jnp.minimum(i, nr[0] - 1)

    def col(i, j, nr):
        return jnp.where(i < nr[0], j, nj - 1)

    grid_spec = pltpu.PrefetchScalarGridSpec(
        num_scalar_prefetch=6,
        grid=(n_tiles, nj),
        in_specs=[
            pl.BlockSpec(memory_space=pl.ANY),
            pl.BlockSpec((None, None, d, tj), lambda i, j, te, tv, tb, nr, tok, dst: (layer, te[tile(i, nr)], 0, col(i, j, nr))),
            pl.BlockSpec((None, None, d, tj), lambda i, j, te, tv, tb, nr, tok, dst: (layer, te[tile(i, nr)], 0, nj + col(i, j, nr))),
            pl.BlockSpec((None, None, 1, tj), lambda i, j, te, tv, tb, nr, tok, dst: (layer, te[tile(i, nr)], 0, col(i, j, nr))),
            pl.BlockSpec((None, None, 1, tj), lambda i, j, te, tv, tb, nr, tok, dst: (layer, te[tile(i, nr)], 0, nj + col(i, j, nr))),
            pl.BlockSpec((None, None, tj, d), lambda i, j, te, tv, tb, nr, tok, dst: (layer, te[tile(i, nr)], col(i, j, nr), 0)),
            pl.BlockSpec((None, None, 1, d), lambda i, j, te, tv, tb, nr, tok, dst: (layer, te[tile(i, nr)], 0, 0)),
        ],
        out_specs=pl.BlockSpec(memory_space=pl.ANY),
        scratch_shapes=[pltpu.VMEM((2, MOE_TILE * gx, LANES), U32), pltpu.VMEM((MOE_TILE, d), F32),
                        pltpu.VMEM((MOE_TILE * gy, LANES), F32),
                        pltpu.VMEM((d, tj), BF16), pltpu.VMEM((d, tj), BF16), pltpu.VMEM((tj, d), BF16),
                        pltpu.SemaphoreType.DMA((2,)), pltpu.SemaphoreType.DMA(())],
    )
    part = MOE_TILE // MOE_ROW_SPLIT
    vmem = (2 * MOE_TILE * half * 4 + 2 * MOE_TILE * d * 4 + 2 * 3 * d * tj * 4 + 3 * d * tj * 2
            + part * (d * 2 + 4 * tj * 4 + d * 4))
    return pl.pallas_call(
        functools.partial(_moe_body, n_tiles=n_tiles, nj=nj),
        grid_spec=grid_spec,
        out_shape=jax.ShapeDtypeStruct((TOP_K * n_tok * gy, LANES), F32),
        compiler_params=pltpu.CompilerParams(dimension_semantics=("arbitrary", "arbitrary"),
                                             vmem_limit_bytes=int(min(VMEM_LIMIT_CAP, vmem + (6 << 20)))),
        name="moe_experts",
    )(tile_expert, tile_valid, tile_base, n_real, tok_sorted, dst_sorted, hn_packed, w1, w1,
      b1.reshape(b1.shape[0], n_e, 1, two_de), b1.reshape(b1.shape[0], n_e, 1, two_de),
      w2, b2.reshape(b2.shape[0], n_e, 1, d))


def _combine_body(x_ref, y_ref, gt_ref, g2_ref, o_ref):
    tt, d = x_ref.shape
    gy = d // LANES
    for grp in range(gy):
        sl = slice(grp * LANES, (grp + 1) * LANES)
        acc = y_ref[0, pl.ds(grp, tt, stride=gy), :] * gt_ref[:, 0:1]
        for k in range(1, TOP_K):
            acc = acc + y_ref[k, pl.ds(grp, tt, stride=gy), :] * gt_ref[:, k:k + 1]
        o_ref[:, sl] = x_ref[:, sl] + g2_ref[:, sl] * acc


def _moe_combine(st, x, y_all, first_row, gates_t, gate2):
    d = x.shape[1]
    gy = d // LANES
    tt = st.row_tile(256)
    assert first_row % tt == 0
    first = first_row // tt
    row_of, col_of = (lambda i: i), (lambda i: 0)
    return pl.pallas_call(
        _combine_body,
        grid=(st.rows // tt,),
        in_specs=[pl.BlockSpec((tt, d), lambda i: (i, 0)),
                  pl.BlockSpec((TOP_K, tt * gy, LANES), lambda i: (0, first + i, 0)),
                  pl.BlockSpec((tt, TOP_K), lambda i: (i, 0)),
                  st.mod_spec(tt, d, row_of, col_of)],
        out_specs=pl.BlockSpec((tt, d), lambda i: (i, 0)),
        out_shape=jax.ShapeDtypeStruct((st.rows, d), F32),
        compiler_params=_params(1, 2 * tt * d * 4 * (3 + TOP_K)),
        name="moe_combine",
    )(x, y_all, gates_t, st.mod_array(gate2))


def _moe_layer(streams, xs, mods, g2, w_router, b_router, w1, b1, w2, b2, layer):
    n_e = w_router.shape[-1]
    w_router_t = w_router[layer].T
    counts = jnp.zeros((n_e, 1), F32)
    hns, idxs, gates, ranks = [], [], [], []
    for st, x, (shift2, scale2, _) in zip(streams, xs, mods):
        hn, idx, gate, rank, counts = _norm_route(st, x, g2, scale2, shift2, w_router_t,
                                                  b_router[layer], counts)
        hns.append(hn), idxs.append(idx), gates.append(gate), ranks.append(rank)
    hn_all = jnp.concatenate(hns, axis=0)
    idx_all = jnp.concatenate(idxs, axis=1)
    rank_all = jnp.concatenate(ranks, axis=1)
    d = xs[0].shape[1]
    gx, gy = d // 2 // LANES, d // LANES
    n_tok = hn_all.shape[0] // gx

    n_assign = n_tok * TOP_K
    n_tiles = n_e + n_assign // MOE_TILE
    cnt = counts[:, 0].astype(I32)
    tiles_e = (cnt + MOE_TILE - 1) // MOE_TILE
    tile_end = jnp.cumsum(tiles_e)
    tile_start = tile_end - tiles_e
    n_real = tile_end[-1:]
    expert_start = jnp.cumsum(cnt) - cnt
    tile_ids = jnp.arange(n_tiles, dtype=I32)
    tile_expert = jnp.minimum(jnp.searchsorted(tile_end, tile_ids, side='right'), n_e - 1).astype(I32)
    tile_off = (tile_ids - tile_start[tile_expert]) * MOE_TILE
    tile_valid = jnp.clip(cnt[tile_expert] - tile_off, 0, MOE_TILE).astype(I32)
    tile_base = jnp.clip(expert_start[tile_expert] + tile_off, 0, n_assign - 1).astype(I32)
    onehot = idx_all[:, :, None] == jnp.arange(n_e, dtype=I32)[None, None, :]
    pos = jnp.sum(jnp.where(onehot, expert_start[None, None, :], 0), axis=-1) + rank_all
    asg_sorted = jnp.zeros((n_assign + MOE_TILE,), I32).at[pos.reshape(-1)].set(jnp.arange(n_assign, dtype=I32))
    tok_sorted = (asg_sorted % n_tok) * gx
    dst_sorted = asg_sorted * gy

    y_all = _moe_experts(hn_all, tok_sorted, dst_sorted, tile_expert, tile_valid, tile_base, n_real,
                         w1, b1, w2, b2, layer).reshape(TOP_K, n_tok * gy, LANES)

    new_xs, off = [], 0
    for st, x, gate, (_, _, gate2) in zip(streams, xs, gates, mods):
        new_xs.append(_moe_combine(st, x, y_all, off, gate.T, gate2))
        off += st.rows
    return new_xs


def _rope_tables(positions, head_dim):
    inv_freq = ROPE_THETA ** (-jnp.arange(0, head_dim, 2, dtype=F32) / head_dim)
    ang = positions.astype(F32)[:, None] * inv_freq[None, :]
    cos, sin = jnp.cos(ang), jnp.sin(ang)
    return jnp.concatenate([cos, cos], axis=-1), jnp.concatenate([-sin, sin], axis=-1)


def kernel(x_prompt, x_sample, c_prompt, c_sample, cache_k_w128, cache_v_w128, cache_k_w512, cache_v_w512, cache_k_w2048, cache_v_w2048, state_rglru_h, state_conv, w_ada, b_ada, g_norm1, g_norm2, w_qkv, w_attn_o, w_rg_in, conv_w, conv_b, w_gate_a, b_gate_a, w_gate_x, b_gate_x, lru_lambda, w_rg_out, w_router, b_router, w_exp1, b_exp1, w_exp2, b_exp2, g_final):
    caches = (cache_k_w128, cache_v_w128, cache_k_w512, cache_v_w512, cache_k_w2048, cache_v_w2048)
    bp, tp, d = x_prompt.shape
    bs, ts, _ = x_sample.shape
    depth = w_ada.shape[0]
    heads, head_dim = cache_k_w128.shape[3], cache_k_w128.shape[4]
    hd = heads * head_dim
    n_groups = len(DILATED_GROUPS)
    dr = w_rg_out.shape[1]
    assert w_qkv.shape[2] == n_groups * 3 * hd

    prompt = _Stream(bp, tp, per_row=False)
    sample = _Stream(bs, ts, per_row=True)
    streams = (prompt, sample)

    n_c = bp + bs
    n_c_pad = -(-n_c // SUBLANES) * SUBLANES
    c_all = jnp.concatenate([c_prompt, c_sample, jnp.zeros((n_c_pad - n_c, d), F32)], axis=0)
    mod = _ada_mod(c_all, w_ada, b_ada).reshape(depth, n_c_pad, 6, d)
    row_range = ((0, bp), (bp, bp + bs))

    cos_p, sin_p = _rope_tables(jnp.arange(tp, dtype=I32), head_dim)
    cos_s, sin_s = _rope_tables(PAST_LEN + jnp.arange(ts, dtype=I32), head_dim)
    rope = ((cos_p, sin_p), (jnp.tile(cos_s, (bs, 1)), jnp.tile(sin_s, (bs, 1))))

    xs = [x_prompt.reshape(bp * tp, d), x_sample.reshape(bs * ts, d)]
    kv_out = [[[] for _ in range(2 * n_groups)] for _ in streams]
    h_out = [[] for _ in streams]
    conv_out = [[] for _ in streams]
    init_h = (None, state_rglru_h)
    init_conv = (None, state_conv)

    for layer in range(depth):
        j = layer // 2
        mods = []
        for si, st in enumerate(streams):
            lo, hi = row_range[si]
            shift1, scale1, gate1, shift2, scale2, gate2 = [mod[layer, lo:hi, n] for n in range(6)]
            mods.append((shift2, scale2, gate2))
            x = xs[si]
            hn = _norm_mod(st, x, g_norm1[layer], scale1, shift1)
            tm = st.row_tile(1024)
            res_ops = lambda tn: [(x, pl.BlockSpec((tm, tn), lambda jj, ii: (ii, jj))),
                                  (st.mod_array(gate1), st.mod_spec(tm, tn, lambda jj, ii: ii, lambda jj, ii: jj))]
            if layer % 2 == 0:
                cos, sin = rope[si]
                per = max(st.seq // tm, 1)
                tab = (pl.BlockSpec((tm, head_dim), lambda jj, ii: (ii % per, 0)) if not st.per_row
                       else pl.BlockSpec((tm, head_dim), lambda jj, ii: (0, 0)))
                qkv = _matmul("qkv_rope", st.rows,
                              [(hn, pl.BlockSpec((tm, d), lambda jj, ii: (ii, 0)))], _a_cast,
                              w_qkv, j, [(cos, tab), (sin, tab)],
                              functools.partial(_e_rope, head_dim=head_dim),
                              tm, hd, F32, extra_vmem=2 * tm * d * 2)
                q3 = qkv.reshape(st.batch, st.seq, n_groups * 3 * hd)
                for g, (window, _) in enumerate(DILATED_GROUPS):
                    keep = st.seq if st.per_row else min(window, st.seq)
                    for which in (1, 2):
                        c0 = (g * 3 + which) * hd
                        kv_out[si][2 * g + which - 1].append(
                            q3[:, st.seq - keep:, c0:c0 + hd].reshape(st.batch, keep, heads, head_dim))
                tn = _pick(d, (1024, 512, 256, 128))
                if not st.per_row:
                    tma = st.row_tile(512)
                    parts = [_band_attention(qkv, st.batch, st.seq, g, heads, head_dim) for g in range(n_groups)]
                    a_ops = ([(o, pl.BlockSpec((tma, hd), lambda jj, ii: (ii, 0))) for o, _ in parts]
                             + [(l, pl.BlockSpec((tma, LANES), lambda jj, ii: (ii, 0))) for _, l in parts])
                    res = [(x, pl.BlockSpec((tma, tn), lambda jj, ii: (ii, jj))),
                           (st.mod_array(gate1), st.mod_spec(tma, tn, lambda jj, ii: ii, lambda jj, ii: jj))]
                    xs[si] = _matmul("attn_out", st.rows, a_ops,
                                     functools.partial(_a_merge, heads=heads, head_dim=head_dim),
                                     w_attn_o, j, res, _e_residual, tma, tn, F32,
                                     extra_vmem=2 * tma * (3 * hd + 3 * LANES + tn) * 4)
                else:
                    comb = _decode_attention(qkv, caches, j, st.batch, st.seq, heads, head_dim)
                    xs[si] = _matmul("attn_out", st.rows,
                                     [(comb, pl.BlockSpec((tm, hd), lambda jj, ii: (ii, 0)))], _a_cast,
                                     w_attn_o, j, res_ops(tn), _e_residual, tm, tn, F32,
                                     extra_vmem=2 * tm * (hd + tn) * 4)
            else:
                tn = _pick(2 * dr, (1024, 512, 256, 128))
                u = _matmul("rg_in", st.rows,
                            [(hn, pl.BlockSpec((tm, d), lambda jj, ii: (ii, 0)))], _a_cast,
                            w_rg_in, j, [], _e_store, tm, tn, F32, extra_vmem=2 * tm * d * 2)
                h0 = jnp.zeros((st.batch, dr), F32) if init_h[si] is None else init_h[si][j]
                cb0 = jnp.zeros((st.batch, CONV_W - 1, dr), F32) if init_conv[si] is None else init_conv[si][j]
                y, h_last, c_new = _rg_core(st, u, cb0, h0, conv_w[j], conv_b[j], w_gate_a[j], b_gate_a[j],
                                            w_gate_x[j], b_gate_x[j], lru_lambda[j],
                                            F32 if st.per_row else BF16)
                h_out[si].append(h_last)
                conv_out[si].append(c_new)
                tn = _pick(d, (1024, 512, 256, 128))
                xs[si] = _matmul("rg_out", st.rows,
                                 [(y, pl.BlockSpec((tm, dr), lambda jj, ii: (ii, 0)))], _a_cast,
                                 w_rg_out, j, res_ops(tn), _e_residual, tm, tn, F32,
                                 extra_vmem=2 * tm * (dr + tn) * 4)
        xs = _moe_layer(streams, xs, mods, g_norm2[layer], w_router, b_router,
                        w_exp1, b_exp1, w_exp2, b_exp2, layer)

    ys = [_final_norm(st, x, g_final).reshape(st.batch, st.seq, d) for st, x in zip(streams, xs)]
    outs = []
    for si in range(2):
        outs.append([jnp.stack(a, axis=0) for a in kv_out[si]]
                    + [jnp.stack(h_out[si], axis=0), jnp.stack(conv_out[si], axis=0)])
    return (ys[0], ys[1], *outs[0], *outs[1])
```

```python
import functools

import numpy as np
import jax
import jax.numpy as jnp
from jax import lax
from jax.experimental import pallas as pl
from jax.experimental.pallas import tpu as pltpu

F32 = jnp.float32
BF16 = jnp.bfloat16
I32 = jnp.int32

DILATED_GROUPS = ((128, 1), (512, 4), (2048, 16))
ROPE_THETA = 10000.0
PAST_LEN = 16384
CONV_W = 4
LRU_C = 8.0
TOP_K = 4
SWIGLU_LIMIT = 7.0
SWIGLU_ALPHA = 1.702
NORM_EPS = 1e-6

V7X_VMEM_BYTES = 64 * 1024 * 1024
VMEM_LIMIT_CAP = 56 * 1024 * 1024
LANES = 128
SUBLANES = 8

MOE_TILE = 1152
MOE_TJ = 256
MOE_ROW_SPLIT = 4
U32 = jnp.uint32


def _pack_halves(x):
    half = x.shape[1] // 2
    lo = lax.bitcast_convert_type(x[:, :half].astype(BF16).astype(F32), U32)
    hi = lax.bitcast_convert_type(x[:, half:].astype(BF16).astype(F32), U32)
    return (lo >> 16) | hi


def _unpack_halves(w):
    lo = lax.bitcast_convert_type(w << 16, F32).astype(BF16)
    hi = lax.bitcast_convert_type(w & U32(0xFFFF0000), F32).astype(BF16)
    return lo, hi


def _params(n_axes, vmem_bytes):
    limit = int(min(VMEM_LIMIT_CAP, max(vmem_bytes * 5 // 4 + (4 << 20), 16 << 20)))
    return pltpu.CompilerParams(dimension_semantics=("arbitrary",) * n_axes,
                                vmem_limit_bytes=limit)


def _pick(n, candidates):
    for c in candidates:
        if n % c == 0:
            return c
    return n


class _Stream:
    def __init__(self, batch, seq, per_row):
        self.batch, self.seq, self.per_row = batch, seq, per_row
        self.rows = batch * seq

    def mod_array(self, vec):
        if self.per_row:
            return jnp.repeat(vec, self.seq, axis=0)
        return vec[:, None, :]

    def mod_spec(self, tm, tn, row_of, col_of):
        if self.per_row:
            return pl.BlockSpec((tm, tn), lambda *g: (row_of(*g), col_of(*g)))
        per_batch = self.seq // tm
        return pl.BlockSpec((None, 1, tn), lambda *g: (row_of(*g) // per_batch, 0, col_of(*g)))

    def row_tile(self, cap):
        if self.per_row:
            return self.rows
        return _pick(self.seq, [c for c in (1024, 512, 256, 128, 64, 32, 16, 8) if c <= cap])


def _ada_body(c_ref, w_ref, b_ref, o_ref):
    c = c_ref[...]
    s = (c * jax.nn.sigmoid(c)).astype(BF16)
    o_ref[0] = jnp.dot(s, w_ref[0].astype(BF16), preferred_element_type=F32) + b_ref[0]


def _ada_mod(c_all, w_ada, b_ada):
    n_layers, d, n6 = w_ada.shape
    mc = c_all.shape[0]
    tn = _pick(n6, (1024, 512, 256, 128))
    vmem = 2 * (mc * d * 4 + d * tn * 4 + tn * 4 + mc * tn * 4) + d * tn * 2
    return pl.pallas_call(
        _ada_body,
        grid=(n_layers, n6 // tn),
        in_specs=[pl.BlockSpec((mc, d), lambda l, j: (0, 0)),
                  pl.BlockSpec((1, d, tn), lambda l, j: (l, 0, j)),
                  pl.BlockSpec((1, 1, tn), lambda l, j: (l, 0, j))],
        out_specs=pl.BlockSpec((1, mc, tn), lambda l, j: (l, 0, j)),
        out_shape=jax.ShapeDtypeStruct((n_layers, mc, n6), F32),
        compiler_params=_params(2, vmem),
        name="ada_mod",
    )(c_all, w_ada, b_ada.reshape(n_layers, 1, n6))


def _rms(x):
    return x * lax.rsqrt(jnp.mean(x * x, axis=-1, keepdims=True) + NORM_EPS)


def _norm_body(x_ref, g_ref, o_ref):
    o_ref[...] = (_rms(x_ref[...]) * g_ref[...]).astype(o_ref.dtype)


def _norm_mod_body(x_ref, g_ref, sc_ref, sh_ref, o_ref):
    hn = (_rms(x_ref[...]) * g_ref[...]) * (1.0 + sc_ref[...]) + sh_ref[...]
    o_ref[...] = hn.astype(o_ref.dtype)


def _final_norm(st, x, g):
    d = x.shape[1]
    tt = st.row_tile(512)
    return pl.pallas_call(
        _norm_body,
        grid=(st.rows // tt,),
        in_specs=[pl.BlockSpec((tt, d), lambda i: (i, 0)),
                  pl.BlockSpec((1, d), lambda i: (0, 0))],
        out_specs=pl.BlockSpec((tt, d), lambda i: (i, 0)),
        out_shape=jax.ShapeDtypeStruct((st.rows, d), F32),
        compiler_params=_params(1, 4 * tt * d * 4),
        name="final_norm",
    )(x, g.reshape(1, d))


def _norm_mod(st, x, g, scale, shift):
    d = x.shape[1]
    tt = st.row_tile(512)
    row_of, col_of = (lambda i: i), (lambda i: 0)
    return pl.pallas_call(
        _norm_mod_body,
        grid=(st.rows // tt,),
        in_specs=[pl.BlockSpec((tt, d), lambda i: (i, 0)),
                  pl.BlockSpec((1, d), lambda i: (0, 0)),
                  st.mod_spec(tt, d, row_of, col_of),
                  st.mod_spec(tt, d, row_of, col_of)],
        out_specs=pl.BlockSpec((tt, d), lambda i: (i, 0)),
        out_shape=jax.ShapeDtypeStruct((st.rows, d), BF16),
        compiler_params=_params(1, 2 * tt * d * (4 + 2) + 6 * tt * d * 4 * int(st.per_row)),
        name="norm_mod",
    )(x, g.reshape(1, d), st.mod_array(scale), st.mod_array(shift))


def _norm_route_body(x_ref, g_ref, sc_ref, sh_ref, wr_ref, br_ref, cin_ref,
                     hn_ref, idx_ref, gate_ref, rank_ref, cnt_ref, carry_ref):
    @pl.when(pl.program_id(0) == 0)
    def _():
        carry_ref[...] = cin_ref[...]

    hn = (_rms(x_ref[...]) * g_ref[...]) * (1.0 + sc_ref[...]) + sh_ref[...]
    packed = _pack_halves(hn)
    groups = packed.shape[1] // LANES
    for grp in range(groups):
        hn_ref[pl.ds(grp, packed.shape[0], stride=groups), :] = packed[:, grp * LANES:(grp + 1) * LANES]
    logits = lax.dot_general(wr_ref[...], hn, (((1,), (1,)), ((), ())),
                             precision=lax.Precision.HIGHEST,
                             preferred_element_type=F32) + br_ref[...]
    n_e, tt = logits.shape
    e_iota = lax.broadcasted_iota(I32, (n_e, tt), 0)
    vals, sels = [], []
    cur = logits
    for k in range(TOP_K):
        m = jnp.max(cur, axis=0, keepdims=True)
        idx = jnp.min(jnp.where(cur == m, e_iota, n_e), axis=0, keepdims=True)
        sel = e_iota == idx
        idx_ref[k:k + 1, :] = idx
        vals.append(m)
        sels.append(sel)
        cur = jnp.where(sel, -jnp.inf, cur)
    exps = [jnp.exp(v - vals[0]) for v in vals]
    den = exps[0]
    for e in exps[1:]:
        den = den + e
    upper = jnp.where(lax.broadcasted_iota(I32, (tt, tt), 0) <= lax.broadcasted_iota(I32, (tt, tt), 1),
                      1.0, 0.0).astype(BF16)
    base = carry_ref[...]
    for k in range(TOP_K):
        gate_ref[k:k + 1, :] = exps[k] / den
        onehot = jnp.where(sels[k], 1.0, 0.0)
        cum = jnp.dot(onehot.astype(BF16), upper, preferred_element_type=F32)
        rank = jnp.sum(onehot * (cum - 1.0 + base), axis=0, keepdims=True)
        rank_ref[k:k + 1, :] = rank.astype(I32)
        base = base + jnp.sum(onehot, axis=1, keepdims=True)
    carry_ref[...] = base
    cnt_ref[...] = base


def _norm_route(st, x, g, scale, shift, w_router_t, b_router, count_in):
    d = x.shape[1]
    n_e = w_router_t.shape[0]
    tt = st.row_tile(512)
    groups = d // 2 // LANES
    row_of, col_of = (lambda i: i), (lambda i: 0)
    tok_spec = pl.BlockSpec((TOP_K, tt), lambda i: (0, i))
    vmem = 2 * tt * d * (4 + 2) + 6 * tt * d * 4 * int(st.per_row) + tt * d * 8 + tt * tt * 8
    return pl.pallas_call(
        _norm_route_body,
        grid=(st.rows // tt,),
        in_specs=[pl.BlockSpec((tt, d), lambda i: (i, 0)),
                  pl.BlockSpec((1, d), lambda i: (0, 0)),
                  st.mod_spec(tt, d, row_of, col_of),
                  st.mod_spec(tt, d, row_of, col_of),
                  pl.BlockSpec((n_e, d), lambda i: (0, 0)),
                  pl.BlockSpec((n_e, 1), lambda i: (0, 0)),
                  pl.BlockSpec((n_e, 1), lambda i: (0, 0))],
        out_specs=[pl.BlockSpec((tt * groups, LANES), lambda i: (i, 0)), tok_spec, tok_spec, tok_spec,
                   pl.BlockSpec((n_e, 1), lambda i: (0, 0))],
        out_shape=[jax.ShapeDtypeStruct((st.rows * groups, LANES), U32),
                   jax.ShapeDtypeStruct((TOP_K, st.rows), I32),
                   jax.ShapeDtypeStruct((TOP_K, st.rows), F32),
                   jax.ShapeDtypeStruct((TOP_K, st.rows), I32),
                   jax.ShapeDtypeStruct((n_e, 1), F32)],
        scratch_shapes=[pltpu.VMEM((n_e, 1), F32)],
        compiler_params=_params(1, vmem),
        name="norm_route",
    )(x, g.reshape(1, d), st.mod_array(scale), st.mod_array(shift), w_router_t,
      b_router.reshape(n_e, 1), count_in)


def _mm_body(*refs, n_a, a_fn, n_e, e_fn):
    a_refs = refs[:n_a]
    w_ref = refs[n_a]
    e_refs = refs[n_a + 1:n_a + 1 + n_e]
    o_ref = refs[n_a + 1 + n_e]
    wbf_ref = refs[n_a + 2 + n_e]

    @pl.when(pl.program_id(1) == 0)
    def _():
        wbf_ref[...] = w_ref[...].astype(BF16)

    acc = jnp.dot(a_fn(*a_refs), wbf_ref[...], preferred_element_type=F32)
    e_fn(acc, o_ref, *e_refs)


def _matmul(name, rows, a_ops, a_fn, w_stack, layer, e_ops, e_fn, tm, tn, out_dtype, extra_vmem=0):
    _, k, n = w_stack.shape
    in_specs = ([s for _, s in a_ops]
                + [pl.BlockSpec((None, k, tn), lambda j, i: (layer, 0, j))]
                + [s for _, s in e_ops])
    vmem = 2 * k * tn * 4 + k * tn * 2 + 2 * tm * tn * 4 + extra_vmem
    return pl.pallas_call(
        functools.partial(_mm_body, n_a=len(a_ops), a_fn=a_fn, n_e=len(e_ops), e_fn=e_fn),
        grid=(n // tn, rows // tm),
        in_specs=in_specs,
        out_specs=pl.BlockSpec((tm, tn), lambda j, i: (i, j)),
        out_shape=jax.ShapeDtypeStruct((rows, n), out_dtype),
        scratch_shapes=[pltpu.VMEM((k, tn), BF16)],
        compiler_params=_params(2, vmem),
        name=name,
    )(*[a for a, _ in a_ops], w_stack, *[e for e, _ in e_ops])


def _a_cast(a_ref):
    return a_ref[...].astype(BF16)


def _e_store(acc, o_ref):
    o_ref[...] = acc.astype(o_ref.dtype)


def _e_residual(acc, o_ref, x_ref, gate_ref):
    o_ref[...] = x_ref[...] + gate_ref[...] * acc


def _e_rope(acc, o_ref, cos_ref, sin_ref, *, head_dim):
    is_v = pl.program_id(0) % 3 == 2

    @pl.when(is_v)
    def _():
        o_ref[...] = acc

    @pl.when(jnp.logical_not(is_v))
    def _():
        cos = cos_ref[...]
        sin = sin_ref[...]
        for h in range(acc.shape[1] // head_dim):
            t = acc[:, h * head_dim:(h + 1) * head_dim]
            o_ref[:, h * head_dim:(h + 1) * head_dim] = t * cos + pltpu.roll(t, head_dim // 2, 1) * sin


def _a_merge(o0_ref, o1_ref, o2_ref, l0_ref, l1_ref, l2_ref, *, heads, head_dim):
    lses = [l0_ref[...], l1_ref[...], l2_ref[...]]
    outs = [o0_ref, o1_ref, o2_ref]
    mx = jnp.maximum(jnp.maximum(lses[0], lses[1]), lses[2])
    es = [jnp.exp(l - mx) for l in lses]
    den = es[0] + es[1] + es[2]
    ws = [e / den for e in es]
    cols = []
    for h in range(heads):
        sl = slice(h * head_dim, (h + 1) * head_dim)
        c = ws[0][:, h:h + 1] * outs[0][:, sl]
        c = c + ws[1][:, h:h + 1] * outs[1][:, sl]
        c = c + ws[2][:, h:h + 1] * outs[2][:, sl]
        cols.append(c.astype(BF16))
    return jnp.concatenate(cols, axis=1)


def _attend(q, k_cat, v_cat, prev_bias, scale):
    nk = q.shape[0]
    s = lax.dot_general(q, k_cat, (((1,), (1,)), ((), ())), preferred_element_type=F32) * scale
    qi = lax.broadcasted_iota(I32, (nk, 2 * nk), 0)
    kj = lax.broadcasted_iota(I32, (nk, 2 * nk), 1)
    s = s + jnp.where(kj < nk, prev_bias, 0.0)
    s = jnp.where(jnp.logical_and(kj >= qi, kj <= qi + nk), s, -jnp.inf)
    m = jnp.max(s, axis=-1, keepdims=True)
    p = jnp.exp(s - m)
    l = jnp.sum(p, axis=-1, keepdims=True)
    o = jnp.dot((p / l).astype(BF16), v_cat, preferred_element_type=F32)
    return o, m + jnp.log(l)


def _band_dense_body(q_ref, kp_ref, kc_ref, vp_ref, vc_ref, o_ref, lse_ref, *, heads, head_dim, nk):
    scale = head_dim ** -0.5
    first_bias = jnp.where(pl.program_id(1) > 0, 0.0, -jnp.inf)
    lane = lax.broadcasted_iota(I32, (nk, LANES), 1)
    for s in range(q_ref.shape[0] // nk):
        rows = slice(s * nk, (s + 1) * nk)
        lse_all = jnp.zeros((nk, LANES), F32)
        for h in range(heads):
            sl = slice(h * head_dim, (h + 1) * head_dim)
            q = q_ref[rows, sl].astype(BF16)
            if s == 0:
                k_cat = jnp.concatenate([kp_ref[:, sl], kc_ref[rows, sl]], axis=0).astype(BF16)
                v_cat = jnp.concatenate([vp_ref[:, sl], vc_ref[rows, sl]], axis=0).astype(BF16)
                bias = first_bias
            else:
                k_cat = kc_ref[(s - 1) * nk:(s + 1) * nk, sl].astype(BF16)
                v_cat = vc_ref[(s - 1) * nk:(s + 1) * nk, sl].astype(BF16)
                bias = 0.0
            o, lse = _attend(q, k_cat, v_cat, bias, scale)
            o_ref[rows, sl] = o
            lse_all = jnp.where(lane == h, lse, lse_all)
        lse_ref[rows, :] = lse_all


def _band_span_body(q_ref, kp_ref, kc_ref, vp_ref, vc_ref, o_ref, lse_ref, *, dil, heads_per_step, head_dim):
    scale = head_dim ** -0.5
    nk = q_ref.shape[0] // dil
    head0 = pl.program_id(2) * heads_per_step
    prev_bias = jnp.where(pl.program_id(1) > 0, 0.0, -jnp.inf)
    lane = lax.broadcasted_iota(I32, (nk, LANES), 1)

    @pl.when(pl.program_id(2) == 0)
    def _():
        lse_ref[...] = jnp.zeros(lse_ref.shape, F32)

    for r in range(dil):
        rows = pl.ds(r, nk, stride=dil)
        lse_all = lse_ref[rows, :]
        for h in range(heads_per_step):
            sl = slice(h * head_dim, (h + 1) * head_dim)
            q = q_ref[rows, sl].astype(BF16)
            k_cat = jnp.concatenate([kp_ref[rows, sl], kc_ref[rows, sl]], axis=0).astype(BF16)
            v_cat = jnp.concatenate([vp_ref[rows, sl], vc_ref[rows, sl]], axis=0).astype(BF16)
            o, lse = _attend(q, k_cat, v_cat, prev_bias, scale)
            o_ref[rows, sl] = o
            lse_all = jnp.where(lane == head0 + h, lse, lse_all)
        lse_ref[rows, :] = lse_all


def _band_attention(qkv, batch, seq, g, heads, head_dim):
    window, dil = DILATED_GROUPS[g]
    nk = window // dil
    hd = heads * head_dim
    rows = batch * seq
    assert seq % window == 0, "prompt length must be a multiple of every window span"
    out_shape = [jax.ShapeDtypeStruct((rows, hd), F32), jax.ShapeDtypeStruct((rows, LANES), F32)]
    if dil == 1:
        tq = _pick(seq, (4 * nk, 2 * nk, nk))
        per, sub = seq // tq, tq // nk
        cur = lambda which: pl.BlockSpec((tq, hd), lambda b, n: (b * per + n, g * 3 + which))
        prev = lambda which: pl.BlockSpec(
            (nk, hd), lambda b, n: (b * per * sub + jnp.maximum(n * sub - 1, 0), g * 3 + which))
        return pl.pallas_call(
            functools.partial(_band_dense_body, heads=heads, head_dim=head_dim, nk=nk),
            grid=(batch, per),
            in_specs=[cur(0), prev(1), cur(1), prev(2), cur(2)],
            out_specs=[pl.BlockSpec((tq, hd), lambda b, n: (b * per + n, 0)),
                       pl.BlockSpec((tq, LANES), lambda b, n: (b * per + n, 0))],
            out_shape=out_shape,
            compiler_params=_params(2, 2 * (4 * tq + 2 * nk) * hd * 4 + (8 << 20)),
            name=f"band_attention_g{g}",
        )(qkv, qkv, qkv, qkv, qkv)
    assert head_dim == LANES
    hps = 1
    n_hs = heads // hps
    per = seq // window
    wide = hps * head_dim

    def spec(which, back):
        return pl.BlockSpec((window, wide),
                            lambda b, n, hs: (b * per + jnp.maximum(n - back, 0), (g * 3 + which) * n_hs + hs))

    return pl.pallas_call(
        functools.partial(_band_span_body, dil=dil, heads_per_step=hps, head_dim=head_dim),
        grid=(batch, per, n_hs),
        in_specs=[spec(0, 0), spec(1, 1), spec(1, 0), spec(2, 1), spec(2, 0)],
        out_specs=[pl.BlockSpec((window, wide), lambda b, n, hs: (b * per + n, hs)),
                   pl.BlockSpec((window, LANES), lambda b, n, hs: (b * per + n, 0))],
        out_shape=out_shape,
        compiler_params=_params(3, 2 * (6 * window * wide + window * LANES) * 4 + (8 << 20)),
        name=f"band_attention_g{g}",
    )(qkv, qkv, qkv, qkv, qkv)


def _kv_rows_body(k_ref, v_ref, ko_ref, vo_ref, *, heads):
    rows = k_ref.shape[0]
    for h in range(heads):
        sl = slice(h * LANES, (h + 1) * LANES)
        ko_ref[pl.ds(h, rows, stride=heads), :] = k_ref[:, sl]
        vo_ref[pl.ds(h, rows, stride=heads), :] = v_ref[:, sl]


def _kv_rows(qkv, batch, seq, keep, g, heads, head_dim):
    assert head_dim == LANES
    hd = heads * head_dim
    tq = _pick(keep, (512, 256, 128, 64, 32, 16, 8))
    assert (seq - keep) % tq == 0
    per, first, per_seq = keep // tq, (seq - keep) // tq, seq // tq
    src = lambda which: pl.BlockSpec((tq, hd), lambda b, n: (b * per_seq + first + n, g * 3 + which))
    dst = pl.BlockSpec((tq * heads, LANES), lambda b, n: (b * per + n, 0))
    shape = jax.ShapeDtypeStruct((batch * keep * heads, LANES), F32)
    k, v = pl.pallas_call(
        functools.partial(_kv_rows_body, heads=heads),
        grid=(batch, per),
        in_specs=[src(1), src(2)],
        out_specs=[dst, dst],
        out_shape=[shape, shape],
        compiler_params=_params(2, 8 * tq * hd * 4),
        name="kv_rows",
    )(qkv, qkv)
    return k.reshape(batch, keep, heads, head_dim), v.reshape(batch, keep, heads, head_dim)


def _decode_body(qkv_ref, *refs, heads, head_dim, seq):
    cache_refs = refs[:-1]
    o_ref = refs[-1]
    hd = heads * head_dim
    scale = head_dim ** -0.5
    nt = (((1,), (1,)), ((), ()))
    diag = (lax.broadcasted_iota(I32, (heads, hd), 0)
            == lax.broadcasted_iota(I32, (heads, hd), 1) // head_dim)
    for t in range(seq):
        outs, lses = [], []
        for g, (window, dil) in enumerate(DILATED_GROUPS):
            nk = window // dil
            base = g * 3 * hd
            q = jnp.broadcast_to(qkv_ref[t:t + 1, base:base + hd], (heads, hd))
            qbd = jnp.where(diag, q, 0.0).astype(BF16)
            k_new = qkv_ref[:, base + hd:base + 2 * hd].astype(BF16)
            v_new = qkv_ref[:, base + 2 * hd:base + 3 * hd].astype(BF16)
            res, first = t % dil, t // dil
            k_old = cache_refs[2 * g][:, res * hd:(res + 1) * hd].astype(BF16)
            v_old = cache_refs[2 * g + 1][:, res * hd:(res + 1) * hd].astype(BF16)
            s_old = lax.dot_general(qbd, k_old, nt, preferred_element_type=F32) * scale
            s_new = lax.dot_general(qbd, k_new, nt, preferred_element_type=F32) * scale
            mi = lax.broadcasted_iota(I32, (heads, nk), 1)
            ti = lax.broadcasted_iota(I32, (heads, seq), 1)
            s_old = jnp.where(mi >= first, s_old, -jnp.inf)
            s_new = jnp.where(jnp.logical_and(ti <= t, ((t - ti) & (dil - 1)) == 0), s_new, -jnp.inf)
            m = jnp.maximum(jnp.max(s_old, axis=-1, keepdims=True), jnp.max(s_new, axis=-1, keepdims=True))
            p_old = jnp.exp(s_old - m)
            p_new = jnp.exp(s_new - m)
            l = jnp.sum(p_old, axis=-1, keepdims=True) + jnp.sum(p_new, axis=-1, keepdims=True)
            o = (jnp.dot((p_old / l).astype(BF16), v_old, preferred_element_type=F32)
                 + jnp.dot((p_new / l).astype(BF16), v_new, preferred_element_type=F32))
            outs.append(o)
            lses.append(m + jnp.log(l))
        mx = jnp.maximum(jnp.maximum(lses[0], lses[1]), lses[2])
        es = [jnp.exp(l - mx) for l in lses]
        den = es[0] + es[1] + es[2]
        comb = (es[0] / den) * outs[0] + (es[1] / den) * outs[1] + (es[2] / den) * outs[2]
        o_ref[t:t + 1, :] = jnp.sum(jnp.where(diag, comb, 0.0), axis=0, keepdims=True)


def _decode_attention(qkv, caches, layer, batch, seq, heads, head_dim):
    hd = heads * head_dim
    ops, specs = [], []
    vmem = seq * qkv.shape[1] * 4
    for g, (window, dil) in enumerate(DILATED_GROUPS):
        assert dil & (dil - 1) == 0 and seq <= window
        nk = window // dil
        for c in caches[2 * g:2 * g + 2]:
            assert c.shape[2] == window, "the cache must hold exactly one window of past positions"
            n_res = min(dil, seq)
            ops.append(c.reshape(c.shape[0], c.shape[1], nk, dil * hd))
            specs.append(pl.BlockSpec((None, None, nk, n_res * hd), lambda b: (layer, b, 0, 0)))
            vmem += nk * n_res * hd * 4
    return pl.pallas_call(
        functools.partial(_decode_body, heads=heads, head_dim=head_dim, seq=seq),
        grid=(batch,),
        in_specs=[pl.BlockSpec((seq, qkv.shape[1]), lambda b: (b, 0))] + specs,
        out_specs=pl.BlockSpec((seq, hd), lambda b: (b, 0)),
        out_shape=jax.ShapeDtypeStruct((batch * seq, hd), F32),
        compiler_params=_params(1, 2 * vmem),
        name="decode_attention",
    )(qkv, *ops)


def _softplus(z):
    return jnp.maximum(z, 0.0) + jnp.log1p(jnp.exp(-jnp.abs(z)))


def _rg_body(gate_ref, xr_ref, cbuf_ref, cw_ref, cb_ref, wga_ref, bga_ref, wgx_ref, bgx_ref,
             lam_ref, h0_ref, y_ref, hlast_ref, cnew_ref, xs_ref, a_ref, b_ref, h_ref,
             *, tt, n_blocks):
    ti = pl.program_id(1)
    halo = CONV_W - 1
    lo = SUBLANES - halo

    @pl.when(ti == 0)
    def _():
        xs_ref[lo:SUBLANES, :] = cbuf_ref[...]
        h_ref[...] = h0_ref[...]

    @pl.when(ti > 0)
    def _():
        xs_ref[lo:SUBLANES, :] = xs_ref[tt + lo:tt + SUBLANES, :]

    xs_ref[SUBLANES:SUBLANES + tt, :] = xr_ref[...]
    conv = cw_ref[0:1, :] * xs_ref[lo:lo + tt, :]
    for tap in range(1, CONV_W):
        conv = conv + cw_ref[tap:tap + 1, :] * xs_ref[lo + tap:lo + tap + tt, :]
    xc = cb_ref[...] + conv

    @pl.when(ti == pl.num_programs(1) - 1)
    def _():
        cnew_ref[...] = xs_ref[tt + lo:tt + SUBLANES, :]

    xcb = xc.astype(BF16)
    bw = xc.shape[1] // n_blocks
    for n in range(n_blocks):
        sl = slice(n * bw, (n + 1) * bw)
        ga = jnp.dot(xcb[:, sl], wga_ref[n].astype(BF16), preferred_element_type=F32) + bga_ref[:, sl]
        gx = jnp.dot(xcb[:, sl], wgx_ref[n].astype(BF16), preferred_element_type=F32) + bgx_ref[:, sl]
        r = jax.nn.sigmoid(ga)
        i = jax.nn.sigmoid(gx)
        log_a = -LRU_C * r * _softplus(-lam_ref[:, sl])
        a = jnp.exp(log_a)
        a_ref[:, sl] = a
        b_ref[:, sl] = jnp.sqrt(-jnp.tanh(log_a) * (a * a + 1.0)) * (i * xc[:, sl])

    sub = lax.broadcasted_iota(I32, (SUBLANES, a_ref.shape[1]), 0)

    def group(gi, h):
        rows = pl.ds(pl.multiple_of(gi * SUBLANES, SUBLANES), SUBLANES)
        ca = a_ref[rows, :]
        cb = b_ref[rows, :]
        for shift in (1, 2, 4):
            pa = pltpu.roll(ca, shift, 0)
            pb = pltpu.roll(cb, shift, 0)
            cb = jnp.where(sub >= shift, cb + ca * pb, cb)
            ca = jnp.where(sub >= shift, ca * pa, ca)
        hs = ca * h + cb
        b_ref[rows, :] = hs
        return hs[SUBLANES - 1:SUBLANES, :]

    h = lax.fori_loop(0, tt // SUBLANES, group, h_ref[...])
    h_ref[...] = h

    @pl.when(ti == pl.num_programs(1) - 1)
    def _():
        hlast_ref[...] = h

    y_ref[...] = (b_ref[...] * jax.nn.gelu(gate_ref[...])).astype(y_ref.dtype)


def _rg_core(st, u, conv_buf, h0, conv_w, conv_b, w_ga, b_ga, w_gx, b_gx, lam, out_dtype):
    dr = u.shape[1] // 2
    n_blocks, bw, _ = w_ga.shape
    batch, seq = st.batch, st.seq
    tt = _pick(seq, (256, 128, 64, 32, 16, 8))
    per = seq // tt
    halo = CONV_W - 1
    vec = lambda: pl.BlockSpec((1, dr), lambda b, t: (0, 0))
    gate_w = lambda: pl.BlockSpec((n_blocks, bw, bw), lambda b, t: (0, 0, 0))
    vmem = 2 * (2 * tt * dr * 4 + 2 * n_blocks * bw * bw * 4 + tt * dr * 4) + 3 * (tt + 8) * dr * 4 + 6 * tt * dr * 4
    y, h_last, c_new = pl.pallas_call(
        functools.partial(_rg_body, tt=tt, n_blocks=n_blocks),
        grid=(batch, per),
        in_specs=[pl.BlockSpec((tt, dr), lambda b, t: (b * per + t, 0)),
                  pl.BlockSpec((tt, dr), lambda b, t: (b * per + t, 1)),
                  pl.BlockSpec((None, halo, dr), lambda b, t: (b, 0, 0)),
                  pl.BlockSpec((CONV_W, dr), lambda b, t: (0, 0)),
                  vec(), gate_w(), vec(), gate_w(), vec(), vec(),
                  pl.BlockSpec((None, 1, dr), lambda b, t: (b, 0, 0))],
        out_specs=[pl.BlockSpec((tt, dr), lambda b, t: (b * per + t, 0)),
                   pl.BlockSpec((None, 1, dr), lambda b, t: (b, 0, 0)),
                   pl.BlockSpec((None, halo, dr), lambda b, t: (b, 0, 0))],
        out_shape=[jax.ShapeDtypeStruct((st.rows, dr), out_dtype),
                   jax.ShapeDtypeStruct((batch, 1, dr), F32),
                   jax.ShapeDtypeStruct((batch, halo, dr), F32)],
        scratch_shapes=[pltpu.VMEM((tt + SUBLANES, dr), F32), pltpu.VMEM((tt, dr), F32),
                        pltpu.VMEM((tt, dr), F32), pltpu.VMEM((1, dr), F32)],
        compiler_params=_params(2, vmem),
        name="rg_core",
    )(u, u, conv_buf, conv_w, conv_b.reshape(1, dr), w_ga, b_ga.reshape(1, dr), w_gx,
      b_gx.reshape(1, dr), lam.reshape(1, dr), h0.reshape(batch, 1, dr))
    return y, h_last.reshape(batch, dr), c_new


def _moe_body(te_ref, tv_ref, tb_ref, nr_ref, tok_ref, dst_ref,
              hn_hbm, w1g_ref, w1l_ref, b1g_ref, b1l_ref, w2_ref, b2_ref, y_hbm,
              xbuf, acc, ystage, wg_s, wl_s, w2_s, gsem, ssem, *, n_tiles, nj):
    i = pl.program_id(0)
    j = pl.program_id(1)
    n_real = nr_ref[0]
    slot = i % 2
    other = 1 - slot
    last = n_real - 1
    tile, d = acc.shape
    gx = xbuf.shape[1] // tile
    gy = d // LANES
    per_gather = tile // nj
    per_scatter = -(-tile // (nj - 1))

    def start_gather(t, s, r):
        tok = pl.multiple_of(tok_ref[tb_ref[t] + r], gx)
        pltpu.make_async_copy(hn_hbm.at[pl.ds(tok, gx)], xbuf.at[s, pl.ds(pl.multiple_of(r * gx, gx), gx)],
                              gsem.at[s]).start()

    def wait_gathers(s):
        pltpu.make_async_copy(hn_hbm.at[pl.ds(0, tile * gx)], xbuf.at[s], gsem.at[s]).wait()

    def start_scatter(t, r):
        dst = pl.multiple_of(dst_ref[tb_ref[t] + r], gy)
        pltpu.make_async_copy(ystage.at[pl.ds(pl.multiple_of(r * gy, gy), gy)], y_hbm.at[pl.ds(dst, gy)],
                              ssem).start(priority=1)

    def wait_scatters(t, enabled):
        n = tv_ref[t]
        rows = 1
        while rows <= tile:
            @pl.when(jnp.logical_and(enabled, (n & rows) != 0))
            def _():
                pltpu.make_async_copy(ystage.at[pl.ds(0, rows * gy)], y_hbm.at[pl.ds(0, rows * gy)], ssem).wait()
            rows *= 2

    def column_step(final):
        nxt = jnp.minimum(i + 1, last)
        prev = jnp.maximum(i - 1, 0)
        n_scatter = 0 if final else per_scatter
        n_batches = 2 * MOE_ROW_SPLIT

        def issue_batch(b):
            for r in range(per_gather * b // n_batches, per_gather * (b + 1) // n_batches):
                start_gather(nxt, other, j * per_gather + r)
            for r in range(n_scatter * b // n_batches, n_scatter * (b + 1) // n_batches):
                rr = j * per_scatter + r

                @pl.when(jnp.logical_and(i >= 1, rr < tv_ref[prev]))
                def _():
                    start_scatter(prev, rr)

        wg_s[...] = w1g_ref[...].astype(BF16)
        wl_s[...] = w1l_ref[...].astype(BF16)
        w2_s[...] = w2_ref[...].astype(BF16)
        part = tile // MOE_ROW_SPLIT
        for p in range(MOE_ROW_SPLIT):
            rows = slice(p * part, (p + 1) * part)
            issue_batch(2 * p)
            halves = [_unpack_halves(xbuf[slot, pl.ds(p * part * gx + grp, part, stride=gx), :])
                      for grp in range(gx)]
            x = jnp.concatenate([lo for lo, _ in halves] + [hi for _, hi in halves], axis=1)
            glu = jnp.dot(x, wg_s[...], preferred_element_type=F32) + b1g_ref[...]
            lin = jnp.dot(x, wl_s[...], preferred_element_type=F32) + b1l_ref[...]
            glu = jnp.minimum(glu, SWIGLU_LIMIT)
            lin = jnp.clip(lin, -SWIGLU_LIMIT, SWIGLU_LIMIT)
            act = glu * jax.nn.sigmoid(SWIGLU_ALPHA * glu) * (lin + 1.0)
            issue_batch(2 * p + 1)
            res = acc[rows, :] + jnp.dot(act.astype(BF16), w2_s[...], preferred_element_type=F32)
            if final:
                for grp in range(gy):
                    ystage[pl.ds(p * part * gy + grp, part, stride=gy), :] = res[:, grp * LANES:(grp + 1) * LANES]
            else:
                acc[rows, :] = res

    @pl.when(i < n_real)
    def _():
        @pl.when(j == 0)
        def _():
            @pl.when(i == 0)
            def _():
                def first_rows(r, carry):
                    start_gather(0, 0, r)
                    return carry
                lax.fori_loop(0, tile, first_rows, 0)

            wait_gathers(slot)
            acc[...] = jnp.broadcast_to(b2_ref[...], acc.shape)

        @pl.when(j < nj - 1)
        def _():
            column_step(False)

        @pl.when(j == nj - 1)
        def _():
            wait_scatters(jnp.maximum(i - 1, 0), i >= 1)
            column_step(True)

    @pl.when(jnp.logical_and(i == n_tiles - 1, j == nj - 1))
    def _():
        def last_rows(r, carry):
            start_scatter(last, r)
            return carry
        lax.fori_loop(0, tv_ref[last], last_rows, 0)
        wait_gathers(n_real % 2)
        wait_scatters(last, True)


def _moe_experts(hn_packed, tok_sorted, dst_sorted, tile_expert, tile_valid, tile_base, n_real, w1, b1, w2, b2, layer):
    _, n_e, d, two_de = w1.shape
    half = d // 2
    gx, gy = half // LANES, d // LANES
    n_tok = hn_packed.shape[0] // gx
    de = two_de // 2
    tj = _pick(de, (MOE_TJ, 128))
    nj = de // tj
    n_tiles = tile_expert.shape[0]
    assert nj >= 2 and MOE_TILE % nj == 0 and MOE_TILE % MOE_ROW_SPLIT == 0 and n_tok >= MOE_TILE

    def tile(i, nr):
        return jnp.minimum(i, nr[0] - 1)

    def col(i, j, nr):
        return jnp.where(i < nr[0], j, nj - 1)

    grid_spec = pltpu.PrefetchScalarGridSpec(
        num_scalar_prefetch=6,
        grid=(n_tiles, nj),
        in_specs=[
            pl.BlockSpec(memory_space=pl.ANY),
            pl.BlockSpec((None, None, d, tj), lambda i, j, te, tv, tb, nr, tok, dst: (layer, te[tile(i, nr)], 0, col(i, j, nr))),
            pl.BlockSpec((None, None, d, tj), lambda i, j, te, tv, tb, nr, tok, dst: (layer, te[tile(i, nr)], 0, nj + col(i, j, nr))),
            pl.BlockSpec((None, None, 1, tj), lambda i, j, te, tv, tb, nr, tok, dst: (layer, te[tile(i, nr)], 0, col(i, j, nr))),
            pl.BlockSpec((None, None, 1, tj), lambda i, j, te, tv, tb, nr, tok, dst: (layer, te[tile(i, nr)], 0, nj + col(i, j, nr))),
            pl.BlockSpec((None, None, tj, d), lambda i, j, te, tv, tb, nr, tok, dst: (layer, te[tile(i, nr)], col(i, j, nr), 0)),
            pl.BlockSpec((None, None, 1, d), lambda i, j, te, tv, tb, nr, tok, dst: (layer, te[tile(i, nr)], 0, 0)),
        ],
        out_specs=pl.BlockSpec(memory_space=pl.ANY),
        scratch_shapes=[pltpu.VMEM((2, MOE_TILE * gx, LANES), U32), pltpu.VMEM((MOE_TILE, d), F32),
                        pltpu.VMEM((MOE_TILE * gy, LANES), F32),
                        pltpu.VMEM((d, tj), BF16), pltpu.VMEM((d, tj), BF16), pltpu.VMEM((tj, d), BF16),
                        pltpu.SemaphoreType.DMA((2,)), pltpu.SemaphoreType.DMA(())],
    )
    part = MOE_TILE // MOE_ROW_SPLIT
    vmem = (2 * MOE_TILE * half * 4 + 2 * MOE_TILE * d * 4 + 2 * 3 * d * tj * 4 + 3 * d * tj * 2
            + part * (d * 2 + 4 * tj * 4 + d * 4))
    return pl.pallas_call(
        functools.partial(_moe_body, n_tiles=n_tiles, nj=nj),
        grid_spec=grid_spec,
        out_shape=jax.ShapeDtypeStruct((TOP_K * n_tok * gy, LANES), F32),
        compiler_params=pltpu.CompilerParams(dimension_semantics=("arbitrary", "arbitrary"),
                                             vmem_limit_bytes=int(min(VMEM_LIMIT_CAP, vmem + (6 << 20)))),
        name="moe_experts",
    )(tile_expert, tile_valid, tile_base, n_real, tok_sorted, dst_sorted, hn_packed, w1, w1,
      b1.reshape(b1.shape[0], n_e, 1, two_de), b1.reshape(b1.shape[0], n_e, 1, two_de),
      w2, b2.reshape(b2.shape[0], n_e, 1, d))


def _combine_body(x_ref, y_ref, gt_ref, g2_ref, o_ref):
    tt, d = x_ref.shape
    gy = d // LANES
    for grp in range(gy):
        sl = slice(grp * LANES, (grp + 1) * LANES)
        acc = y_ref[0, pl.ds(grp, tt, stride=gy), :] * gt_ref[:, 0:1]
        for k in range(1, TOP_K):
            acc = acc + y_ref[k, pl.ds(grp, tt, stride=gy), :] * gt_ref[:, k:k + 1]
        o_ref[:, sl] = x_ref[:, sl] + g2_ref[:, sl] * acc


def _moe_combine(st, x, y_all, first_row, gates_t, gate2):
    d = x.shape[1]
    gy = d // LANES
    tt = st.row_tile(256)
    assert first_row % tt == 0
    first = first_row // tt
    row_of, col_of = (lambda i: i), (lambda i: 0)
    return pl.pallas_call(
        _combine_body,
        grid=(st.rows // tt,),
        in_specs=[pl.BlockSpec((tt, d), lambda i: (i, 0)),
                  pl.BlockSpec((TOP_K, tt * gy, LANES), lambda i: (0, first + i, 0)),
                  pl.BlockSpec((tt, TOP_K), lambda i: (i, 0)),
                  st.mod_spec(tt, d, row_of, col_of)],
        out_specs=pl.BlockSpec((tt, d), lambda i: (i, 0)),
        out_shape=jax.ShapeDtypeStruct((st.rows, d), F32),
        compiler_params=_params(1, 2 * tt * d * 4 * (3 + TOP_K)),
        name="moe_combine",
    )(x, y_all, gates_t, st.mod_array(gate2))


def _moe_layer(streams, xs, mods, g2, w_router, b_router, w1, b1, w2, b2, layer):
    n_e = w_router.shape[-1]
    w_router_t = w_router[layer].T
    counts = jnp.zeros((n_e, 1), F32)
    hns, idxs, gates, ranks = [], [], [], []
    for st, x, (shift2, scale2, _) in zip(streams, xs, mods):
        hn, idx, gate, rank, counts = _norm_route(st, x, g2, scale2, shift2, w_router_t,
                                                  b_router[layer], counts)
        hns.append(hn), idxs.append(idx), gates.append(gate), ranks.append(rank)
    hn_all = jnp.concatenate(hns, axis=0)
    idx_all = jnp.concatenate(idxs, axis=1)
    rank_all = jnp.concatenate(ranks, axis=1)
    d = xs[0].shape[1]
    gx, gy = d // 2 // LANES, d // LANES
    n_tok = hn_all.shape[0] // gx

    n_assign = n_tok * TOP_K
    n_tiles = n_e + n_assign // MOE_TILE
    cnt = counts[:, 0].astype(I32)
    tiles_e = (cnt + MOE_TILE - 1) // MOE_TILE
    tile_end = jnp.cumsum(tiles_e)
    tile_start = tile_end - tiles_e
    n_real = tile_end[-1:]
    expert_start = jnp.cumsum(cnt) - cnt
    tile_ids = jnp.arange(n_tiles, dtype=I32)
    tile_expert = jnp.minimum(jnp.searchsorted(tile_end, tile_ids, side='right'), n_e - 1).astype(I32)
    tile_off = (tile_ids - tile_start[tile_expert]) * MOE_TILE
    tile_valid = jnp.clip(cnt[tile_expert] - tile_off, 0, MOE_TILE).astype(I32)
    tile_base = jnp.clip(expert_start[tile_expert] + tile_off, 0, n_assign - 1).astype(I32)
    onehot = idx_all[:, :, None] == jnp.arange(n_e, dtype=I32)[None, None, :]
    pos = jnp.sum(jnp.where(onehot, expert_start[None, None, :], 0), axis=-1) + rank_all
    asg_sorted = jnp.zeros((n_assign + MOE_TILE,), I32).at[pos.reshape(-1)].set(jnp.arange(n_assign, dtype=I32))
    tok_sorted = (asg_sorted % n_tok) * gx
    dst_sorted = asg_sorted * gy

    y_all = _moe_experts(hn_all, tok_sorted, dst_sorted, tile_expert, tile_valid, tile_base, n_real,
                         w1, b1, w2, b2, layer).reshape(TOP_K, n_tok * gy, LANES)

    new_xs, off = [], 0
    for st, x, gate, (_, _, gate2) in zip(streams, xs, gates, mods):
        new_xs.append(_moe_combine(st, x, y_all, off, gate.T, gate2))
        off += st.rows
    return new_xs


def _rope_tables(positions, head_dim):
    inv_freq = ROPE_THETA ** (-jnp.arange(0, head_dim, 2, dtype=F32) / head_dim)
    ang = positions.astype(F32)[:, None] * inv_freq[None, :]
    cos, sin = jnp.cos(ang), jnp.sin(ang)
    return jnp.concatenate([cos, cos], axis=-1), jnp.concatenate([-sin, sin], axis=-1)


def kernel(x_prompt, x_sample, c_prompt, c_sample, cache_k_w128, cache_v_w128, cache_k_w512, cache_v_w512, cache_k_w2048, cache_v_w2048, state_rglru_h, state_conv, w_ada, b_ada, g_norm1, g_norm2, w_qkv, w_attn_o, w_rg_in, conv_w, conv_b, w_gate_a, b_gate_a, w_gate_x, b_gate_x, lru_lambda, w_rg_out, w_router, b_router, w_exp1, b_exp1, w_exp2, b_exp2, g_final):
    caches = (cache_k_w128, cache_v_w128, cache_k_w512, cache_v_w512, cache_k_w2048, cache_v_w2048)
    bp, tp, d = x_prompt.shape
    bs, ts, _ = x_sample.shape
    depth = w_ada.shape[0]
    heads, head_dim = cache_k_w128.shape[3], cache_k_w128.shape[4]
    hd = heads * head_dim
    n_groups = len(DILATED_GROUPS)
    dr = w_rg_out.shape[1]
    assert w_qkv.shape[2] == n_groups * 3 * hd

    prompt = _Stream(bp, tp, per_row=False)
    sample = _Stream(bs, ts, per_row=True)
    streams = (prompt, sample)

    n_c = bp + bs
    n_c_pad = -(-n_c // SUBLANES) * SUBLANES
    c_all = jnp.concatenate([c_prompt, c_sample, jnp.zeros((n_c_pad - n_c, d), F32)], axis=0)
    mod = _ada_mod(c_all, w_ada, b_ada).reshape(depth, n_c_pad, 6, d)
    row_range = ((0, bp), (bp, bp + bs))

    cos_p, sin_p = _rope_tables(jnp.arange(tp, dtype=I32), head_dim)
    cos_s, sin_s = _rope_tables(PAST_LEN + jnp.arange(ts, dtype=I32), head_dim)
    rope = ((cos_p, sin_p), (jnp.tile(cos_s, (bs, 1)), jnp.tile(sin_s, (bs, 1))))

    xs = [x_prompt.reshape(bp * tp, d), x_sample.reshape(bs * ts, d)]
    kv_out = [[[] for _ in range(2 * n_groups)] for _ in streams]
    h_out = [[] for _ in streams]
    conv_out = [[] for _ in streams]
    init_h = (None, state_rglru_h)
    init_conv = (None, state_conv)

    for layer in range(depth):
        j = layer // 2
        mods = []
        for si, st in enumerate(streams):
            lo, hi = row_range[si]
            shift1, scale1, gate1, shift2, scale2, gate2 = [mod[layer, lo:hi, n] for n in range(6)]
            mods.append((shift2, scale2, gate2))
            x = xs[si]
            hn = _norm_mod(st, x, g_norm1[layer], scale1, shift1)
            tm = st.row_tile(1024)
            res_ops = lambda tn: [(x, pl.BlockSpec((tm, tn), lambda jj, ii: (ii, jj))),
                                  (st.mod_array(gate1), st.mod_spec(tm, tn, lambda jj, ii: ii, lambda jj, ii: jj))]
            if layer % 2 == 0:
                cos, sin = rope[si]
                per = max(st.seq // tm, 1)
                tab = (pl.BlockSpec((tm, head_dim), lambda jj, ii: (ii % per, 0)) if not st.per_row
                       else pl.BlockSpec((tm, head_dim), lambda jj, ii: (0, 0)))
                qkv = _matmul("qkv_rope", st.rows,
                              [(hn, pl.BlockSpec((tm, d), lambda jj, ii: (ii, 0)))], _a_cast,
                              w_qkv, j, [(cos, tab), (sin, tab)],
                              functools.partial(_e_rope, head_dim=head_dim),
                              tm, hd, F32, extra_vmem=2 * tm * d * 2)
                for g, (window, _) in enumerate(DILATED_GROUPS):
                    keep = st.seq if st.per_row else min(window, st.seq)
                    k_new, v_new = _kv_rows(qkv, st.batch, st.seq, keep, g, heads, head_dim)
                    kv_out[si][2 * g].append(k_new)
                    kv_out[si][2 * g + 1].append(v_new)
                tn = _pick(d, (1024, 512, 256, 128))
                if not st.per_row:
                    tma = st.row_tile(512)
                    parts = [_band_attention(qkv, st.batch, st.seq, g, heads, head_dim) for g in range(n_groups)]
                    a_ops = ([(o, pl.BlockSpec((tma, hd), lambda jj, ii: (ii, 0))) for o, _ in parts]
                             + [(l, pl.BlockSpec((tma, LANES), lambda jj, ii: (ii, 0))) for _, l in parts])
                    res = [(x, pl.BlockSpec((tma, tn), lambda jj, ii: (ii, jj))),
                           (st.mod_array(gate1), st.mod_spec(tma, tn, lambda jj, ii: ii, lambda jj, ii: jj))]
                    xs[si] = _matmul("attn_out", st.rows, a_ops,
                                     functools.partial(_a_merge, heads=heads, head_dim=head_dim),
                                     w_attn_o, j, res, _e_residual, tma, tn, F32,
                                     extra_vmem=2 * tma * (3 * hd + 3 * LANES + tn) * 4)
                else:
                    comb = _decode_attention(qkv, caches, j, st.batch, st.seq, heads, head_dim)
                    xs[si] = _matmul("attn_out", st.rows,
                                     [(comb, pl.BlockSpec((tm, hd), lambda jj, ii: (ii, 0)))], _a_cast,
                                     w_attn_o, j, res_ops(tn), _e_residual, tm, tn, F32,
                                     extra_vmem=2 * tm * (hd + tn) * 4)
            else:
                tn = _pick(2 * dr, (1024, 512, 256, 128))
                u = _matmul("rg_in", st.rows,
                            [(hn, pl.BlockSpec((tm, d), lambda jj, ii: (ii, 0)))], _a_cast,
                            w_rg_in, j, [], _e_store, tm, tn, F32, extra_vmem=2 * tm * d * 2)
                h0 = jnp.zeros((st.batch, dr), F32) if init_h[si] is None else init_h[si][j]
                cb0 = jnp.zeros((st.batch, CONV_W - 1, dr), F32) if init_conv[si] is None else init_conv[si][j]
                y, h_last, c_new = _rg_core(st, u, cb0, h0, conv_w[j], conv_b[j], w_gate_a[j], b_gate_a[j],
                                            w_gate_x[j], b_gate_x[j], lru_lambda[j],
                                            F32 if st.per_row else BF16)
                h_out[si].append(h_last)
                conv_out[si].append(c_new)
                tn = _pick(d, (1024, 512, 256, 128))
                xs[si] = _matmul("rg_out", st.rows,
                                 [(y, pl.BlockSpec((tm, dr), lambda jj, ii: (ii, 0)))], _a_cast,
                                 w_rg_out, j, res_ops(tn), _e_residual, tm, tn, F32,
                                 extra_vmem=2 * tm * (dr + tn) * 4)
        xs = _moe_layer(streams, xs, mods, g_norm2[layer], w_router, b_router,
                        w_exp1, b_exp1, w_exp2, b_exp2, layer)

    ys = [_final_norm(st, x, g_final).reshape(st.batch, st.seq, d) for st, x in zip(streams, xs)]
    outs = []
    for si in range(2):
        outs.append([jnp.stack(a, axis=0) for a in kv_out[si]]
                    + [jnp.stack(h_out[si], axis=0), jnp.stack(conv_out[si], axis=0)])
    return (ys[0], ys[1], *outs[0], *outs[1])
```

```python
import functools

import numpy as np
import jax
import jax.numpy as jnp
from jax import lax
from jax.experimental import pallas as pl
from jax.experimental.pallas import tpu as pltpu

F32 = jnp.float32
BF16 = jnp.bfloat16
I32 = jnp.int32

DILATED_GROUPS = ((128, 1), (512, 4), (2048, 16))
ROPE_THETA = 10000.0
PAST_LEN = 16384
CONV_W = 4
LRU_C = 8.0
TOP_K = 4
SWIGLU_LIMIT = 7.0
SWIGLU_ALPHA = 1.702
NORM_EPS = 1e-6

V7X_VMEM_BYTES = 64 * 1024 * 1024
VMEM_LIMIT_CAP = 56 * 1024 * 1024
LANES = 128
SUBLANES = 8

MOE_TILE = 1152
MOE_TJ = 256
MOE_ROW_SPLIT = 4
U32 = jnp.uint32


def _pack_halves(x):
    half = x.shape[1] // 2
    lo = lax.bitcast_convert_type(x[:, :half].astype(BF16).astype(F32), U32)
    hi = lax.bitcast_convert_type(x[:, half:].astype(BF16).astype(F32), U32)
    return (lo >> 16) | hi


def _unpack_halves(w):
    lo = lax.bitcast_convert_type(w << 16, F32).astype(BF16)
    hi = lax.bitcast_convert_type(w & U32(0xFFFF0000), F32).astype(BF16)
    return lo, hi


def _params(n_axes, vmem_bytes):
    limit = int(min(VMEM_LIMIT_CAP, max(vmem_bytes * 5 // 4 + (4 << 20), 16 << 20)))
    return pltpu.CompilerParams(dimension_semantics=("arbitrary",) * n_axes,
                                vmem_limit_bytes=limit)


def _pick(n, candidates):
    for c in candidates:
        if n % c == 0:
            return c
    return n


class _Stream:
    def __init__(self, batch, seq, per_row):
        self.batch, self.seq, self.per_row = batch, seq, per_row
        self.rows = batch * seq

    def mod_array(self, vec):
        if self.per_row:
            return jnp.repeat(vec, self.seq, axis=0)
        return vec[:, None, :]

    def mod_spec(self, tm, tn, row_of, col_of):
        if self.per_row:
            return pl.BlockSpec((tm, tn), lambda *g: (row_of(*g), col_of(*g)))
        per_batch = self.seq // tm
        return pl.BlockSpec((None, 1, tn), lambda *g: (row_of(*g) // per_batch, 0, col_of(*g)))

    def row_tile(self, cap):
        if self.per_row:
            return self.rows
        return _pick(self.seq, [c for c in (1024, 512, 256, 128, 64, 32, 16, 8) if c <= cap])


def _ada_body(c_ref, w_ref, b_ref, o_ref):
    c = c_ref[...]
    s = (c * jax.nn.sigmoid(c)).astype(BF16)
    o_ref[0] = jnp.dot(s, w_ref[0].astype(BF16), preferred_element_type=F32) + b_ref[0]


def _ada_mod(c_all, w_ada, b_ada):
    n_layers, d, n6 = w_ada.shape
    mc = c_all.shape[0]
    tn = _pick(n6, (1024, 512, 256, 128))
    vmem = 2 * (mc * d * 4 + d * tn * 4 + tn * 4 + mc * tn * 4) + d * tn * 2
    return pl.pallas_call(
        _ada_body,
        grid=(n_layers, n6 // tn),
        in_specs=[pl.BlockSpec((mc, d), lambda l, j: (0, 0)),
                  pl.BlockSpec((1, d, tn), lambda l, j: (l, 0, j)),
                  pl.BlockSpec((1, 1, tn), lambda l, j: (l, 0, j))],
        out_specs=pl.BlockSpec((1, mc, tn), lambda l, j: (l, 0, j)),
        out_shape=jax.ShapeDtypeStruct((n_layers, mc, n6), F32),
        compiler_params=_params(2, vmem),
        name="ada_mod",
    )(c_all, w_ada, b_ada.reshape(n_layers, 1, n6))


def _rms(x):
    return x * lax.rsqrt(jnp.mean(x * x, axis=-1, keepdims=True) + NORM_EPS)


def _norm_body(x_ref, g_ref, o_ref):
    o_ref[...] = (_rms(x_ref[...]) * g_ref[...]).astype(o_ref.dtype)


def _norm_mod_body(x_ref, g_ref, sc_ref, sh_ref, o_ref):
    hn = (_rms(x_ref[...]) * g_ref[...]) * (1.0 + sc_ref[...]) + sh_ref[...]
    o_ref[...] = hn.astype(o_ref.dtype)


def _final_norm(st, x, g):
    d = x.shape[1]
    tt = st.row_tile(512)
    return pl.pallas_call(
        _norm_body,
        grid=(st.rows // tt,),
        in_specs=[pl.BlockSpec((tt, d), lambda i: (i, 0)),
                  pl.BlockSpec((1, d), lambda i: (0, 0))],
        out_specs=pl.BlockSpec((tt, d), lambda i: (i, 0)),
        out_shape=jax.ShapeDtypeStruct((st.rows, d), F32),
        compiler_params=_params(1, 4 * tt * d * 4),
        name="final_norm",
    )(x, g.reshape(1, d))


def _norm_mod(st, x, g, scale, shift):
    d = x.shape[1]
    tt = st.row_tile(512)
    row_of, col_of = (lambda i: i), (lambda i: 0)
    return pl.pallas_call(
        _norm_mod_body,
        grid=(st.rows // tt,),
        in_specs=[pl.BlockSpec((tt, d), lambda i: (i, 0)),
                  pl.BlockSpec((1, d), lambda i: (0, 0)),
                  st.mod_spec(tt, d, row_of, col_of),
                  st.mod_spec(tt, d, row_of, col_of)],
        out_specs=pl.BlockSpec((tt, d), lambda i: (i, 0)),
        out_shape=jax.ShapeDtypeStruct((st.rows, d), BF16),
        compiler_params=_params(1, 2 * tt * d * (4 + 2) + 6 * tt * d * 4 * int(st.per_row)),
        name="norm_mod",
    )(x, g.reshape(1, d), st.mod_array(scale), st.mod_array(shift))


def _norm_route_body(x_ref, g_ref, sc_ref, sh_ref, wr_ref, br_ref, cin_ref,
                     hn_ref, idx_ref, gate_ref, rank_ref, cnt_ref, carry_ref):
    @pl.when(pl.program_id(0) == 0)
    def _():
        carry_ref[...] = cin_ref[...]

    hn = (_rms(x_ref[...]) * g_ref[...]) * (1.0 + sc_ref[...]) + sh_ref[...]
    packed = _pack_halves(hn)
    groups = packed.shape[1] // LANES
    for grp in range(groups):
        hn_ref[pl.ds(grp, packed.shape[0], stride=groups), :] = packed[:, grp * LANES:(grp + 1) * LANES]
    logits = lax.dot_general(wr_ref[...], hn, (((1,), (1,)), ((), ())),
                             precision=lax.Precision.HIGHEST,
                             preferred_element_type=F32) + br_ref[...]
    n_e, tt = logits.shape
    e_iota = lax.broadcasted_iota(I32, (n_e, tt), 0)
    vals, sels = [], []
    cur = logits
    for k in range(TOP_K):
        m = jnp.max(cur, axis=0, keepdims=True)
        idx = jnp.min(jnp.where(cur == m, e_iota, n_e), axis=0, keepdims=True)
        sel = e_iota == idx
        idx_ref[k:k + 1, :] = idx
        vals.append(m)
        sels.append(sel)
        cur = jnp.where(sel, -jnp.inf, cur)
    exps = [jnp.exp(v - vals[0]) for v in vals]
    den = exps[0]
    for e in exps[1:]:
        den = den + e
    upper = jnp.where(lax.broadcasted_iota(I32, (tt, tt), 0) <= lax.broadcasted_iota(I32, (tt, tt), 1),
                      1.0, 0.0).astype(BF16)
    base = carry_ref[...]
    for k in range(TOP_K):
        gate_ref[k:k + 1, :] = exps[k] / den
        onehot = jnp.where(sels[k], 1.0, 0.0)
        cum = jnp.dot(onehot.astype(BF16), upper, preferred_element_type=F32)
        rank = jnp.sum(onehot * (cum - 1.0 + base), axis=0, keepdims=True)
        rank_ref[k:k + 1, :] = rank.astype(I32)
        base = base + jnp.sum(onehot, axis=1, keepdims=True)
    carry_ref[...] = base
    cnt_ref[...] = base


def _norm_route(st, x, g, scale, shift, w_router_t, b_router, count_in):
    d = x.shape[1]
    n_e = w_router_t.shape[0]
    tt = st.row_tile(512)
    groups = d // 2 // LANES
    row_of, col_of = (lambda i: i), (lambda i: 0)
    tok_spec = pl.BlockSpec((TOP_K, tt), lambda i: (0, i))
    vmem = 2 * tt * d * (4 + 2) + 6 * tt * d * 4 * int(st.per_row) + tt * d * 8 + tt * tt * 8
    return pl.pallas_call(
        _norm_route_body,
        grid=(st.rows // tt,),
        in_specs=[pl.BlockSpec((tt, d), lambda i: (i, 0)),
                  pl.BlockSpec((1, d), lambda i: (0, 0)),
                  st.mod_spec(tt, d, row_of, col_of),
                  st.mod_spec(tt, d, row_of, col_of),
                  pl.BlockSpec((n_e, d), lambda i: (0, 0)),
                  pl.BlockSpec((n_e, 1), lambda i: (0, 0)),
                  pl.BlockSpec((n_e, 1), lambda i: (0, 0))],
        out_specs=[pl.BlockSpec((tt * groups, LANES), lambda i: (i, 0)), tok_spec, tok_spec, tok_spec,
                   pl.BlockSpec((n_e, 1), lambda i: (0, 0))],
        out_shape=[jax.ShapeDtypeStruct((st.rows * groups, LANES), U32),
                   jax.ShapeDtypeStruct((TOP_K, st.rows), I32),
                   jax.ShapeDtypeStruct((TOP_K, st.rows), F32),
                   jax.ShapeDtypeStruct((TOP_K, st.rows), I32),
                   jax.ShapeDtypeStruct((n_e, 1), F32)],
        scratch_shapes=[pltpu.VMEM((n_e, 1), F32)],
        compiler_params=_params(1, vmem),
        name="norm_route",
    )(x, g.reshape(1, d), st.mod_array(scale), st.mod_array(shift), w_router_t,
      b_router.reshape(n_e, 1), count_in)


def _mm_body(*refs, n_a, a_fn, n_e, e_fn):
    a_refs = refs[:n_a]
    w_ref = refs[n_a]
    e_refs = refs[n_a + 1:n_a + 1 + n_e]
    o_ref = refs[n_a + 1 + n_e]
    wbf_ref = refs[n_a + 2 + n_e]

    @pl.when(pl.program_id(1) == 0)
    def _():
        wbf_ref[...] = w_ref[...].astype(BF16)

    acc = jnp.dot(a_fn(*a_refs), wbf_ref[...], preferred_element_type=F32)
    e_fn(acc, o_ref, *e_refs)


def _matmul(name, rows, a_ops, a_fn, w_stack, layer, e_ops, e_fn, tm, tn, out_dtype, extra_vmem=0):
    _, k, n = w_stack.shape
    in_specs = ([s for _, s in a_ops]
                + [pl.BlockSpec((None, k, tn), lambda j, i: (layer, 0, j))]
                + [s for _, s in e_ops])
    vmem = 2 * k * tn * 4 + k * tn * 2 + 2 * tm * tn * 4 + extra_vmem
    return pl.pallas_call(
        functools.partial(_mm_body, n_a=len(a_ops), a_fn=a_fn, n_e=len(e_ops), e_fn=e_fn),
        grid=(n // tn, rows // tm),
        in_specs=in_specs,
        out_specs=pl.BlockSpec((tm, tn), lambda j, i: (i, j)),
        out_shape=jax.ShapeDtypeStruct((rows, n), out_dtype),
        scratch_shapes=[pltpu.VMEM((k, tn), BF16)],
        compiler_params=_params(2, vmem),
        name=name,
    )(*[a for a, _ in a_ops], w_stack, *[e for e, _ in e_ops])


def _a_cast(a_ref):
    return a_ref[...].astype(BF16)


def _e_store(acc, o_ref):
    o_ref[...] = acc.astype(o_ref.dtype)


def _e_residual(acc, o_ref, x_ref, gate_ref):
    o_ref[...] = x_ref[...] + gate_ref[...] * acc


def _e_rope(acc, o_ref, cos_ref, sin_ref, *, head_dim):
    is_v = pl.program_id(0) % 3 == 2

    @pl.when(is_v)
    def _():
        o_ref[...] = acc

    @pl.when(jnp.logical_not(is_v))
    def _():
        cos = cos_ref[...]
        sin = sin_ref[...]
        for h in range(acc.shape[1] // head_dim):
            t = acc[:, h * head_dim:(h + 1) * head_dim]
            o_ref[:, h * head_dim:(h + 1) * head_dim] = t * cos + pltpu.roll(t, head_dim // 2, 1) * sin


def _a_merge(o0_ref, o1_ref, o2_ref, l0_ref, l1_ref, l2_ref, *, heads, head_dim):
    lses = [l0_ref[...], l1_ref[...], l2_ref[...]]
    outs = [o0_ref, o1_ref, o2_ref]
    mx = jnp.maximum(jnp.maximum(lses[0], lses[1]), lses[2])
    es = [jnp.exp(l - mx) for l in lses]
    den = es[0] + es[1] + es[2]
    ws = [e / den for e in es]
    cols = []
    for h in range(heads):
        sl = slice(h * head_dim, (h + 1) * head_dim)
        c = ws[0][:, h:h + 1] * outs[0][:, sl]
        c = c + ws[1][:, h:h + 1] * outs[1][:, sl]
        c = c + ws[2][:, h:h + 1] * outs[2][:, sl]
        cols.append(c.astype(BF16))
    return jnp.concatenate(cols, axis=1)


def _attend(q, k_cat, v_cat, prev_bias, scale):
    nk = q.shape[0]
    s = lax.dot_general(q, k_cat, (((1,), (1,)), ((), ())), preferred_element_type=F32) * scale
    qi = lax.broadcasted_iota(I32, (nk, 2 * nk), 0)
    kj = lax.broadcasted_iota(I32, (nk, 2 * nk), 1)
    s = s + jnp.where(kj < nk, prev_bias, 0.0)
    s = jnp.where(jnp.logical_and(kj >= qi, kj <= qi + nk), s, -jnp.inf)
    m = jnp.max(s, axis=-1, keepdims=True)
    p = jnp.exp(s - m)
    l = jnp.sum(p, axis=-1, keepdims=True)
    o = jnp.dot((p / l).astype(BF16), v_cat, preferred_element_type=F32)
    return o, m + jnp.log(l)


def _band_dense_body(q_ref, kp_ref, kc_ref, vp_ref, vc_ref, o_ref, lse_ref, *, heads, head_dim, nk):
    scale = head_dim ** -0.5
    first_bias = jnp.where(pl.program_id(1) > 0, 0.0, -jnp.inf)
    lane = lax.broadcasted_iota(I32, (nk, LANES), 1)
    for s in range(q_ref.shape[0] // nk):
        rows = slice(s * nk, (s + 1) * nk)
        lse_all = jnp.zeros((nk, LANES), F32)
        for h in range(heads):
            sl = slice(h * head_dim, (h + 1) * head_dim)
            q = q_ref[rows, sl].astype(BF16)
            if s == 0:
                k_cat = jnp.concatenate([kp_ref[:, sl], kc_ref[rows, sl]], axis=0).astype(BF16)
                v_cat = jnp.concatenate([vp_ref[:, sl], vc_ref[rows, sl]], axis=0).astype(BF16)
                bias = first_bias
            else:
                k_cat = kc_ref[(s - 1) * nk:(s + 1) * nk, sl].astype(BF16)
                v_cat = vc_ref[(s - 1) * nk:(s + 1) * nk, sl].astype(BF16)
                bias = 0.0
            o, lse = _attend(q, k_cat, v_cat, bias, scale)
            o_ref[rows, sl] = o
            lse_all = jnp.where(lane == h, lse, lse_all)
        lse_ref[rows, :] = lse_all


def _band_span_body(q_ref, kp_ref, kc_ref, vp_ref, vc_ref, o_ref, lse_ref, *, dil, heads_per_step, head_dim):
    scale = head_dim ** -0.5
    nk = q_ref.shape[0] // dil
    head0 = pl.program_id(2) * heads_per_step
    prev_bias = jnp.where(pl.program_id(1) > 0, 0.0, -jnp.inf)
    lane = lax.broadcasted_iota(I32, (nk, LANES), 1)

    @pl.when(pl.program_id(2) == 0)
    def _():
        lse_ref[...] = jnp.zeros(lse_ref.shape, F32)

    for r in range(dil):
        rows = pl.ds(r, nk, stride=dil)
        lse_all = lse_ref[rows, :]
        for h in range(heads_per_step):
            sl = slice(h * head_dim, (h + 1) * head_dim)
            q = q_ref[rows, sl].astype(BF16)
            k_cat = jnp.concatenate([kp_ref[rows, sl], kc_ref[rows, sl]], axis=0).astype(BF16)
            v_cat = jnp.concatenate([vp_ref[rows, sl], vc_ref[rows, sl]], axis=0).astype(BF16)
            o, lse = _attend(q, k_cat, v_cat, prev_bias, scale)
            o_ref[rows, sl] = o
            lse_all = jnp.where(lane == head0 + h, lse, lse_all)
        lse_ref[rows, :] = lse_all


def _band_attention(qkv, batch, seq, g, heads, head_dim):
    window, dil = DILATED_GROUPS[g]
    nk = window // dil
    hd = heads * head_dim
    rows = batch * seq
    assert seq % window == 0, "prompt length must be a multiple of every window span"
    out_shape = [jax.ShapeDtypeStruct((rows, hd), F32), jax.ShapeDtypeStruct((rows, LANES), F32)]
    if dil == 1:
        tq = _pick(seq, (4 * nk, 2 * nk, nk))
        per, sub = seq // tq, tq // nk
        cur = lambda which: pl.BlockSpec((tq, hd), lambda b, n: (b * per + n, g * 3 + which))
        prev = lambda which: pl.BlockSpec(
            (nk, hd), lambda b, n: (b * per * sub + jnp.maximum(n * sub - 1, 0), g * 3 + which))
        return pl.pallas_call(
            functools.partial(_band_dense_body, heads=heads, head_dim=head_dim, nk=nk),
            grid=(batch, per),
            in_specs=[cur(0), prev(1), cur(1), prev(2), cur(2)],
            out_specs=[pl.BlockSpec((tq, hd), lambda b, n: (b * per + n, 0)),
                       pl.BlockSpec((tq, LANES), lambda b, n: (b * per + n, 0))],
            out_shape=out_shape,
            compiler_params=_params(2, 2 * (4 * tq + 2 * nk) * hd * 4 + (8 << 20)),
            name=f"band_attention_g{g}",
        )(qkv, qkv, qkv, qkv, qkv)
    assert head_dim == LANES
    hps = 1
    n_hs = heads // hps
    per = seq // window
    wide = hps * head_dim

    def spec(which, back):
        return pl.BlockSpec((window, wide),
                            lambda b, n, hs: (b * per + jnp.maximum(n - back, 0), (g * 3 + which) * n_hs + hs))

    return pl.pallas_call(
        functools.partial(_band_span_body, dil=dil, heads_per_step=hps, head_dim=head_dim),
        grid=(batch, per, n_hs),
        in_specs=[spec(0, 0), spec(1, 1), spec(1, 0), spec(2, 1), spec(2, 0)],
        out_specs=[pl.BlockSpec((window, wide), lambda b, n, hs: (b * per + n, hs)),
                   pl.BlockSpec((window, LANES), lambda b, n, hs: (b * per + n, 0))],
        out_shape=out_shape,
        compiler_params=_params(3, 2 * (6 * window * wide + window * LANES) * 4 + (8 << 20)),
        name=f"band_attention_g{g}",
    )(qkv, qkv, qkv, qkv, qkv)


def _kv_rows_body(k_ref, v_ref, ko_ref, vo_ref, *, heads):
    for h in range(heads):
        sl = slice(h * LANES, (h + 1) * LANES)
        ko_ref[:, h, :] = k_ref[:, sl]
        vo_ref[:, h, :] = v_ref[:, sl]


def _kv_rows(qkv, batch, seq, keep, g, heads, head_dim):
    assert head_dim == LANES
    hd = heads * head_dim
    tq = _pick(keep, (512, 256, 128, 64, 32, 16, 8))
    assert (seq - keep) % tq == 0
    per, first, per_seq = keep // tq, (seq - keep) // tq, seq // tq
    src = lambda which: pl.BlockSpec((tq, hd), lambda b, n: (b * per_seq + first + n, g * 3 + which))
    dst = pl.BlockSpec((None, tq, heads, head_dim), lambda b, n: (b, n, 0, 0))
    shape = jax.ShapeDtypeStruct((batch, keep, heads, head_dim), F32)
    return pl.pallas_call(
        functools.partial(_kv_rows_body, heads=heads),
        grid=(batch, per),
        in_specs=[src(1), src(2)],
        out_specs=[dst, dst],
        out_shape=[shape, shape],
        compiler_params=_params(2, 8 * tq * hd * 4),
        name="kv_rows",
    )(qkv, qkv)


def _decode_body(qkv_ref, *refs, heads, head_dim, seq):
    cache_refs = refs[:-1]
    o_ref = refs[-1]
    hd = heads * head_dim
    scale = head_dim ** -0.5
    nt = (((1,), (1,)), ((), ()))
    diag = (lax.broadcasted_iota(I32, (heads, hd), 0)
            == lax.broadcasted_iota(I32, (heads, hd), 1) // head_dim)
    for t in range(seq):
        outs, lses = [], []
        for g, (window, dil) in enumerate(DILATED_GROUPS):
            nk = window // dil
            base = g * 3 * hd
            q = jnp.broadcast_to(qkv_ref[t:t + 1, base:base + hd], (heads, hd))
            qbd = jnp.where(diag, q, 0.0).astype(BF16)
            k_new = qkv_ref[:, base + hd:base + 2 * hd].astype(BF16)
            v_new = qkv_ref[:, base + 2 * hd:base + 3 * hd].astype(BF16)
            res, first = t % dil, t // dil
            k_old = cache_refs[2 * g][:, res * hd:(res + 1) * hd].astype(BF16)
            v_old = cache_refs[2 * g + 1][:, res * hd:(res + 1) * hd].astype(BF16)
            s_old = lax.dot_general(qbd, k_old, nt, preferred_element_type=F32) * scale
            s_new = lax.dot_general(qbd, k_new, nt, preferred_element_type=F32) * scale
            mi = lax.broadcasted_iota(I32, (heads, nk), 1)
            ti = lax.broadcasted_iota(I32, (heads, seq), 1)
            s_old = jnp.where(mi >= first, s_old, -jnp.inf)
            s_new = jnp.where(jnp.logical_and(ti <= t, ((t - ti) & (dil - 1)) == 0), s_new, -jnp.inf)
            m = jnp.maximum(jnp.max(s_old, axis=-1, keepdims=True), jnp.max(s_new, axis=-1, keepdims=True))
            p_old = jnp.exp(s_old - m)
            p_new = jnp.exp(s_new - m)
            l = jnp.sum(p_old, axis=-1, keepdims=True) + jnp.sum(p_new, axis=-1, keepdims=True)
            o = (jnp.dot((p_old / l).astype(BF16), v_old, preferred_element_type=F32)
                 + jnp.dot((p_new / l).astype(BF16), v_new, preferred_element_type=F32))
            outs.append(o)
            lses.append(m + jnp.log(l))
        mx = jnp.maximum(jnp.maximum(lses[0], lses[1]), lses[2])
        es = [jnp.exp(l - mx) for l in lses]
        den = es[0] + es[1] + es[2]
        comb = (es[0] / den) * outs[0] + (es[1] / den) * outs[1] + (es[2] / den) * outs[2]
        o_ref[t:t + 1, :] = jnp.sum(jnp.where(diag, comb, 0.0), axis=0, keepdims=True)


def _decode_attention(qkv, caches, layer, batch, seq, heads, head_dim):
    hd = heads * head_dim
    ops, specs = [], []
    vmem = seq * qkv.shape[1] * 4
    for g, (window, dil) in enumerate(DILATED_GROUPS):
        assert dil & (dil - 1) == 0 and seq <= window
        nk = window // dil
        for c in caches[2 * g:2 * g + 2]:
            assert c.shape[2] == window, "the cache must hold exactly one window of past positions"
            n_res = min(dil, seq)
            ops.append(c.reshape(c.shape[0], c.shape[1], nk, dil * hd))
            specs.append(pl.BlockSpec((None, None, nk, n_res * hd), lambda b: (layer, b, 0, 0)))
            vmem += nk * n_res * hd * 4
    return pl.pallas_call(
        functools.partial(_decode_body, heads=heads, head_dim=head_dim, seq=seq),
        grid=(batch,),
        in_specs=[pl.BlockSpec((seq, qkv.shape[1]), lambda b: (b, 0))] + specs,
        out_specs=pl.BlockSpec((seq, hd), lambda b: (b, 0)),
        out_shape=jax.ShapeDtypeStruct((batch * seq, hd), F32),
        compiler_params=_params(1, 2 * vmem),
        name="decode_attention",
    )(qkv, *ops)


def _softplus(z):
    return jnp.maximum(z, 0.0) + jnp.log1p(jnp.exp(-jnp.abs(z)))


def _rg_body(gate_ref, xr_ref, cbuf_ref, cw_ref, cb_ref, wga_ref, bga_ref, wgx_ref, bgx_ref,
             lam_ref, h0_ref, y_ref, hlast_ref, cnew_ref, xs_ref, a_ref, b_ref, h_ref,
             *, tt, n_blocks):
    ti = pl.program_id(1)
    halo = CONV_W - 1
    lo = SUBLANES - halo

    @pl.when(ti == 0)
    def _():
        xs_ref[lo:SUBLANES, :] = cbuf_ref[...]
        h_ref[...] = h0_ref[...]

    @pl.when(ti > 0)
    def _():
        xs_ref[lo:SUBLANES, :] = xs_ref[tt + lo:tt + SUBLANES, :]

    xs_ref[SUBLANES:SUBLANES + tt, :] = xr_ref[...]
    conv = cw_ref[0:1, :] * xs_ref[lo:lo + tt, :]
    for tap in range(1, CONV_W):
        conv = conv + cw_ref[tap:tap + 1, :] * xs_ref[lo + tap:lo + tap + tt, :]
    xc = cb_ref[...] + conv

    @pl.when(ti == pl.num_programs(1) - 1)
    def _():
        cnew_ref[...] = xs_ref[tt + lo:tt + SUBLANES, :]

    xcb = xc.astype(BF16)
    bw = xc.shape[1] // n_blocks
    for n in range(n_blocks):
        sl = slice(n * bw, (n + 1) * bw)
        ga = jnp.dot(xcb[:, sl], wga_ref[n].astype(BF16), preferred_element_type=F32) + bga_ref[:, sl]
        gx = jnp.dot(xcb[:, sl], wgx_ref[n].astype(BF16), preferred_element_type=F32) + bgx_ref[:, sl]
        r = jax.nn.sigmoid(ga)
        i = jax.nn.sigmoid(gx)
        log_a = -LRU_C * r * _softplus(-lam_ref[:, sl])
        a = jnp.exp(log_a)
        a_ref[:, sl] = a
        b_ref[:, sl] = jnp.sqrt(-jnp.tanh(log_a) * (a * a + 1.0)) * (i * xc[:, sl])

    sub = lax.broadcasted_iota(I32, (SUBLANES, a_ref.shape[1]), 0)

    def group(gi, h):
        rows = pl.ds(pl.multiple_of(gi * SUBLANES, SUBLANES), SUBLANES)
        ca = a_ref[rows, :]
        cb = b_ref[rows, :]
        for shift in (1, 2, 4):
            pa = pltpu.roll(ca, shift, 0)
            pb = pltpu.roll(cb, shift, 0)
            cb = jnp.where(sub >= shift, cb + ca * pb, cb)
            ca = jnp.where(sub >= shift, ca * pa, ca)
        hs = ca * h + cb
        b_ref[rows, :] = hs
        return hs[SUBLANES - 1:SUBLANES, :]

    h = lax.fori_loop(0, tt // SUBLANES, group, h_ref[...])
    h_ref[...] = h

    @pl.when(ti == pl.num_programs(1) - 1)
    def _():
        hlast_ref[...] = h

    y_ref[...] = (b_ref[...] * jax.nn.gelu(gate_ref[...])).astype(y_ref.dtype)


def _rg_core(st, u, conv_buf, h0, conv_w, conv_b, w_ga, b_ga, w_gx, b_gx, lam, out_dtype):
    dr = u.shape[1] // 2
    n_blocks, bw, _ = w_ga.shape
    batch, seq = st.batch, st.seq
    tt = _pick(seq, (256, 128, 64, 32, 16, 8))
    per = seq // tt
    halo = CONV_W - 1
    vec = lambda: pl.BlockSpec((1, dr), lambda b, t: (0, 0))
    gate_w = lambda: pl.BlockSpec((n_blocks, bw, bw), lambda b, t: (0, 0, 0))
    vmem = 2 * (2 * tt * dr * 4 + 2 * n_blocks * bw * bw * 4 + tt * dr * 4) + 3 * (tt + 8) * dr * 4 + 6 * tt * dr * 4
    y, h_last, c_new = pl.pallas_call(
        functools.partial(_rg_body, tt=tt, n_blocks=n_blocks),
        grid=(batch, per),
        in_specs=[pl.BlockSpec((tt, dr), lambda b, t: (b * per + t, 0)),
                  pl.BlockSpec((tt, dr), lambda b, t: (b * per + t, 1)),
                  pl.BlockSpec((None, halo, dr), lambda b, t: (b, 0, 0)),
                  pl.BlockSpec((CONV_W, dr), lambda b, t: (0, 0)),
                  vec(), gate_w(), vec(), gate_w(), vec(), vec(),
                  pl.BlockSpec((None, 1, dr), lambda b, t: (b, 0, 0))],
        out_specs=[pl.BlockSpec((tt, dr), lambda b, t: (b * per + t, 0)),
                   pl.BlockSpec((None, 1, dr), lambda b, t: (b, 0, 0)),
                   pl.BlockSpec((None, halo, dr), lambda b, t: (b, 0, 0))],
        out_shape=[jax.ShapeDtypeStruct((st.rows, dr), out_dtype),
                   jax.ShapeDtypeStruct((batch, 1, dr), F32),
                   jax.ShapeDtypeStruct((batch, halo, dr), F32)],
        scratch_shapes=[pltpu.VMEM((tt + SUBLANES, dr), F32), pltpu.VMEM((tt, dr), F32),
                        pltpu.VMEM((tt, dr), F32), pltpu.VMEM((1, dr), F32)],
        compiler_params=_params(2, vmem),
        name="rg_core",
    )(u, u, conv_buf, conv_w, conv_b.reshape(1, dr), w_ga, b_ga.reshape(1, dr), w_gx,
      b_gx.reshape(1, dr), lam.reshape(1, dr), h0.reshape(batch, 1, dr))
    return y, h_last.reshape(batch, dr), c_new


def _moe_body(te_ref, tv_ref, tb_ref, nr_ref, tok_ref, dst_ref,
              hn_hbm, w1g_ref, w1l_ref, b1g_ref, b1l_ref, w2_ref, b2_ref, y_hbm,
              xbuf, xbf, acc, ystage, wg_s, wl_s, w2_s, gsem, ssem, *, n_tiles, nj):
    i = pl.program_id(0)
    j = pl.program_id(1)
    n_real = nr_ref[0]
    slot = i % 2
    other = 1 - slot
    last = n_real - 1
    tile, d = acc.shape
    gx = xbuf.shape[1] // tile
    gy = d // LANES
    per_gather = tile // nj
    per_scatter = -(-tile // (nj - 1))

    def start_gather(t, s, r):
        tok = pl.multiple_of(tok_ref[tb_ref[t] + r], gx)
        pltpu.make_async_copy(hn_hbm.at[pl.ds(tok, gx)], xbuf.at[s, pl.ds(pl.multiple_of(r * gx, gx), gx)],
                              gsem.at[s]).start()

    def wait_gathers(s):
        pltpu.make_async_copy(hn_hbm.at[pl.ds(0, tile * gx)], xbuf.at[s], gsem.at[s]).wait()

    def start_scatter(t, r):
        dst = pl.multiple_of(dst_ref[tb_ref[t] + r], gy)
        pltpu.make_async_copy(ystage.at[pl.ds(pl.multiple_of(r * gy, gy), gy)], y_hbm.at[pl.ds(dst, gy)],
                              ssem).start(priority=1)

    def wait_scatters(t, enabled):
        n = tv_ref[t]
        rows = 1
        while rows <= tile:
            @pl.when(jnp.logical_and(enabled, (n & rows) != 0))
            def _():
                pltpu.make_async_copy(ystage.at[pl.ds(0, rows * gy)], y_hbm.at[pl.ds(0, rows * gy)], ssem).wait()
            rows *= 2

    def column_step(final):
        nxt = jnp.minimum(i + 1, last)
        prev = jnp.maximum(i - 1, 0)
        n_scatter = 0 if final else per_scatter
        n_batches = 2 * MOE_ROW_SPLIT

        def issue_batch(b):
            for r in range(per_gather * b // n_batches, per_gather * (b + 1) // n_batches):
                start_gather(nxt, other, j * per_gather + r)
            for r in range(n_scatter * b // n_batches, n_scatter * (b + 1) // n_batches):
                rr = j * per_scatter + r

                @pl.when(jnp.logical_and(i >= 1, rr < tv_ref[prev]))
                def _():
                    start_scatter(prev, rr)

        wg_s[...] = w1g_ref[...].astype(BF16)
        wl_s[...] = w1l_ref[...].astype(BF16)
        w2_s[...] = w2_ref[...].astype(BF16)
        part = tile // MOE_ROW_SPLIT
        for p in range(MOE_ROW_SPLIT):
            rows = slice(p * part, (p + 1) * part)
            issue_batch(2 * p)
            x = xbf[rows, :]
            glu = jnp.dot(x, wg_s[...], preferred_element_type=F32) + b1g_ref[...]
            lin = jnp.dot(x, wl_s[...], preferred_element_type=F32) + b1l_ref[...]
            glu = jnp.minimum(glu, SWIGLU_LIMIT)
            lin = jnp.clip(lin, -SWIGLU_LIMIT, SWIGLU_LIMIT)
            act = glu * jax.nn.sigmoid(SWIGLU_ALPHA * glu) * (lin + 1.0)
            issue_batch(2 * p + 1)
            res = acc[rows, :] + jnp.dot(act.astype(BF16), w2_s[...], preferred_element_type=F32)
            if final:
                for grp in range(gy):
                    ystage[pl.ds(p * part * gy + grp, part, stride=gy), :] = res[:, grp * LANES:(grp + 1) * LANES]
            else:
                acc[rows, :] = res

    @pl.when(i < n_real)
    def _():
        @pl.when(j == 0)
        def _():
            @pl.when(i == 0)
            def _():
                def first_rows(r, carry):
                    start_gather(0, 0, r)
                    return carry
                lax.fori_loop(0, tile, first_rows, 0)

            wait_gathers(slot)
            half = d // 2
            for grp in range(gx):
                lo, hi = _unpack_halves(xbuf[slot, pl.ds(grp, tile, stride=gx), :])
                xbf[:, grp * LANES:(grp + 1) * LANES] = lo
                xbf[:, half + grp * LANES:half + (grp + 1) * LANES] = hi
            acc[...] = jnp.broadcast_to(b2_ref[...], acc.shape)

        @pl.when(j < nj - 1)
        def _():
            column_step(False)

        @pl.when(j == nj - 1)
        def _():
            wait_scatters(jnp.maximum(i - 1, 0), i >= 1)
            column_step(True)

    @pl.when(jnp.logical_and(i == n_tiles - 1, j == nj - 1))
    def _():
        def last_rows(r, carry):
            start_scatter(last, r)
            return carry
        lax.fori_loop(0, tv_ref[last], last_rows, 0)
        wait_gathers(n_real % 2)
        wait_scatters(last, True)


def _moe_experts(hn_packed, tok_sorted, dst_sorted, tile_expert, tile_valid, tile_base, n_real, w1, b1, w2, b2, layer):
    _, n_e, d, two_de = w1.shape
    half = d // 2
    gx, gy = half // LANES, d // LANES
    n_tok = hn_packed.shape[0] // gx
    de = two_de // 2
    tj = _pick(de, (MOE_TJ, 128))
    nj = de // tj
    n_tiles = tile_expert.shape[0]
    assert nj >= 2 and MOE_TILE % nj == 0 and MOE_TILE % MOE_ROW_SPLIT == 0 and n_tok >= MOE_TILE

    def tile(i, nr):
        return jnp.minimum(i, nr[0] - 1)

    def col(i, j, nr):
        return jnp.where(i < nr[0], j, nj - 1)

    grid_spec = pltpu.PrefetchScalarGridSpec(
        num_scalar_prefetch=6,
        grid=(n_tiles, nj),
        in_specs=[
            pl.BlockSpec(memory_space=pl.ANY),
            pl.BlockSpec((None, None, d, tj), lambda i, j, te, tv, tb, nr, tok, dst: (layer, te[tile(i, nr)], 0, col(i, j, nr))),
            pl.BlockSpec((None, None, d, tj), lambda i, j, te, tv, tb, nr, tok, dst: (layer, te[tile(i, nr)], 0, nj + col(i, j, nr))),
            pl.BlockSpec((None, None, 1, tj), lambda i, j, te, tv, tb, nr, tok, dst: (layer, te[tile(i, nr)], 0, col(i, j, nr))),
            pl.BlockSpec((None, None, 1, tj), lambda i, j, te, tv, tb, nr, tok, dst: (layer, te[tile(i, nr)], 0, nj + col(i, j, nr))),
            pl.BlockSpec((None, None, tj, d), lambda i, j, te, tv, tb, nr, tok, dst: (layer, te[tile(i, nr)], col(i, j, nr), 0)),
            pl.BlockSpec((None, None, 1, d), lambda i, j, te, tv, tb, nr, tok, dst: (layer, te[tile(i, nr)], 0, 0)),
        ],
        out_specs=pl.BlockSpec(memory_space=pl.ANY),
        scratch_shapes=[pltpu.VMEM((2, MOE_TILE * gx, LANES), U32), pltpu.VMEM((MOE_TILE, d), BF16),
                        pltpu.VMEM((MOE_TILE, d), F32),
                        pltpu.VMEM((MOE_TILE * gy, LANES), F32),
                        pltpu.VMEM((d, tj), BF16), pltpu.VMEM((d, tj), BF16), pltpu.VMEM((tj, d), BF16),
                        pltpu.SemaphoreType.DMA((2,)), pltpu.SemaphoreType.DMA(())],
    )
    part = MOE_TILE // MOE_ROW_SPLIT
    vmem = (2 * MOE_TILE * half * 4 + MOE_TILE * d * 2 + 2 * MOE_TILE * d * 4 + 2 * 3 * d * tj * 4 + 3 * d * tj * 2
            + part * (4 * tj * 4 + d * 4))
    return pl.pallas_call(
        functools.partial(_moe_body, n_tiles=n_tiles, nj=nj),
        grid_spec=grid_spec,
        out_shape=jax.ShapeDtypeStruct((TOP_K * n_tok * gy, LANES), F32),
        compiler_params=pltpu.CompilerParams(dimension_semantics=("arbitrary", "arbitrary"),
                                             vmem_limit_bytes=int(min(VMEM_LIMIT_CAP, vmem + (6 << 20)))),
        name="moe_experts",
    )(tile_expert, tile_valid, tile_base, n_real, tok_sorted, dst_sorted, hn_packed, w1, w1,
      b1.reshape(b1.shape[0], n_e, 1, two_de), b1.reshape(b1.shape[0], n_e, 1, two_de),
      w2, b2.reshape(b2.shape[0], n_e, 1, d))


def _combine_body(x_ref, y_ref, gt_ref, g2_ref, o_ref):
    tt, d = x_ref.shape
    gy = d // LANES
    for grp in range(gy):
        sl = slice(grp * LANES, (grp + 1) * LANES)
        acc = y_ref[0, pl.ds(grp, tt, stride=gy), :] * gt_ref[:, 0:1]
        for k in range(1, TOP_K):
            acc = acc + y_ref[k, pl.ds(grp, tt, stride=gy), :] * gt_ref[:, k:k + 1]
        o_ref[:, sl] = x_ref[:, sl] + g2_ref[:, sl] * acc


def _moe_combine(st, x, y_all, first_row, gates_t, gate2):
    d = x.shape[1]
    gy = d // LANES
    tt = st.row_tile(256)
    assert first_row % tt == 0
    first = first_row // tt
    row_of, col_of = (lambda i: i), (lambda i: 0)
    return pl.pallas_call(
        _combine_body,
        grid=(st.rows // tt,),
        in_specs=[pl.BlockSpec((tt, d), lambda i: (i, 0)),
                  pl.BlockSpec((TOP_K, tt * gy, LANES), lambda i: (0, first + i, 0)),
                  pl.BlockSpec((tt, TOP_K), lambda i: (i, 0)),
                  st.mod_spec(tt, d, row_of, col_of)],
        out_specs=pl.BlockSpec((tt, d), lambda i: (i, 0)),
        out_shape=jax.ShapeDtypeStruct((st.rows, d), F32),
        compiler_params=_params(1, 2 * tt * d * 4 * (3 + TOP_K)),
        name="moe_combine",
    )(x, y_all, gates_t, st.mod_array(gate2))


def _moe_layer(streams, xs, mods, g2, w_router, b_router, w1, b1, w2, b2, layer):
    n_e = w_router.shape[-1]
    w_router_t = w_router[layer].T
    counts = jnp.zeros((n_e, 1), F32)
    hns, idxs, gates, ranks = [], [], [], []
    for st, x, (shift2, scale2, _) in zip(streams, xs, mods):
        hn, idx, gate, rank, counts = _norm_route(st, x, g2, scale2, shift2, w_router_t,
                                                  b_router[layer], counts)
        hns.append(hn), idxs.append(idx), gates.append(gate), ranks.append(rank)
    hn_all = jnp.concatenate(hns, axis=0)
    idx_all = jnp.concatenate(idxs, axis=1)
    rank_all = jnp.concatenate(ranks, axis=1)
    d = xs[0].shape[1]
    gx, gy = d // 2 // LANES, d // LANES
    n_tok = hn_all.shape[0] // gx

    n_assign = n_tok * TOP_K
    n_tiles = n_e + n_assign // MOE_TILE
    cnt = counts[:, 0].astype(I32)
    tiles_e = (cnt + MOE_TILE - 1) // MOE_TILE
    tile_end = jnp.cumsum(tiles_e)
    tile_start = tile_end - tiles_e
    n_real = tile_end[-1:]
    expert_start = jnp.cumsum(cnt) - cnt
    tile_ids = jnp.arange(n_tiles, dtype=I32)
    tile_expert = jnp.minimum(jnp.searchsorted(tile_end, tile_ids, side='right'), n_e - 1).astype(I32)
    tile_off = (tile_ids - tile_start[tile_expert]) * MOE_TILE
    tile_valid = jnp.clip(cnt[tile_expert] - tile_off, 0, MOE_TILE).astype(I32)
    tile_base = jnp.clip(expert_start[tile_expert] + tile_off, 0, n_assign - 1).astype(I32)
    onehot = idx_all[:, :, None] == jnp.arange(n_e, dtype=I32)[None, None, :]
    pos = jnp.sum(jnp.where(onehot, expert_start[None, None, :], 0), axis=-1) + rank_all
    asg_sorted = jnp.concatenate([jnp.argsort(pos.reshape(-1)).astype(I32), jnp.zeros((MOE_TILE,), I32)])
    tok_sorted = (asg_sorted % n_tok) * gx
    dst_sorted = asg_sorted * gy

    y_all = _moe_experts(hn_all, tok_sorted, dst_sorted, tile_expert, tile_valid, tile_base, n_real,
                         w1, b1, w2, b2, layer).reshape(TOP_K, n_tok * gy, LANES)

    new_xs, off = [], 0
    for st, x, gate, (_, _, gate2) in zip(streams, xs, gates, mods):
        new_xs.append(_moe_combine(st, x, y_all, off, gate.T, gate2))
        off += st.rows
    return new_xs


def _rope_tables(positions, head_dim):
    inv_freq = ROPE_THETA ** (-jnp.arange(0, head_dim, 2, dtype=F32) / head_dim)
    ang = positions.astype(F32)[:, None] * inv_freq[None, :]
    cos, sin = jnp.cos(ang), jnp.sin(ang)
    return jnp.concatenate([cos, cos], axis=-1), jnp.concatenate([-sin, sin], axis=-1)


def kernel(x_prompt, x_sample, c_prompt, c_sample, cache_k_w128, cache_v_w128, cache_k_w512, cache_v_w512, cache_k_w2048, cache_v_w2048, state_rglru_h, state_conv, w_ada, b_ada, g_norm1, g_norm2, w_qkv, w_attn_o, w_rg_in, conv_w, conv_b, w_gate_a, b_gate_a, w_gate_x, b_gate_x, lru_lambda, w_rg_out, w_router, b_router, w_exp1, b_exp1, w_exp2, b_exp2, g_final):
    caches = (cache_k_w128, cache_v_w128, cache_k_w512, cache_v_w512, cache_k_w2048, cache_v_w2048)
    bp, tp, d = x_prompt.shape
    bs, ts, _ = x_sample.shape
    depth = w_ada.shape[0]
    heads, head_dim = cache_k_w128.shape[3], cache_k_w128.shape[4]
    hd = heads * head_dim
    n_groups = len(DILATED_GROUPS)
    dr = w_rg_out.shape[1]
    assert w_qkv.shape[2] == n_groups * 3 * hd

    prompt = _Stream(bp, tp, per_row=False)
    sample = _Stream(bs, ts, per_row=True)
    streams = (prompt, sample)

    n_c = bp + bs
    n_c_pad = -(-n_c // SUBLANES) * SUBLANES
    c_all = jnp.concatenate([c_prompt, c_sample, jnp.zeros((n_c_pad - n_c, d), F32)], axis=0)
    mod = _ada_mod(c_all, w_ada, b_ada).reshape(depth, n_c_pad, 6, d)
    row_range = ((0, bp), (bp, bp + bs))

    cos_p, sin_p = _rope_tables(jnp.arange(tp, dtype=I32), head_dim)
    cos_s, sin_s = _rope_tables(PAST_LEN + jnp.arange(ts, dtype=I32), head_dim)
    rope = ((cos_p, sin_p), (jnp.tile(cos_s, (bs, 1)), jnp.tile(sin_s, (bs, 1))))

    xs = [x_prompt.reshape(bp * tp, d), x_sample.reshape(bs * ts, d)]
    kv_out = [[[] for _ in range(2 * n_groups)] for _ in streams]
    h_out = [[] for _ in streams]
    conv_out = [[] for _ in streams]
    init_h = (None, state_rglru_h)
    init_conv = (None, state_conv)

    for layer in range(depth):
        j = layer // 2
        mods = []
        for si, st in enumerate(streams):
            lo, hi = row_range[si]
            shift1, scale1, gate1, shift2, scale2, gate2 = [mod[layer, lo:hi, n] for n in range(6)]
            mods.append((shift2, scale2, gate2))
            x = xs[si]
            hn = _norm_mod(st, x, g_norm1[layer], scale1, shift1)
            tm = st.row_tile(1024)
            res_ops = lambda tn: [(x, pl.BlockSpec((tm, tn), lambda jj, ii: (ii, jj))),
                                  (st.mod_array(gate1), st.mod_spec(tm, tn, lambda jj, ii: ii, lambda jj, ii: jj))]
            if layer % 2 == 0:
                cos, sin = rope[si]
                per = max(st.seq // tm, 1)
                tab = (pl.BlockSpec((tm, head_dim), lambda jj, ii: (ii % per, 0)) if not st.per_row
                       else pl.BlockSpec((tm, head_dim), lambda jj, ii: (0, 0)))
                qkv = _matmul("qkv_rope", st.rows,
                              [(hn, pl.BlockSpec((tm, d), lambda jj, ii: (ii, 0)))], _a_cast,
                              w_qkv, j, [(cos, tab), (sin, tab)],
                              functools.partial(_e_rope, head_dim=head_dim),
                              tm, hd, F32, extra_vmem=2 * tm * d * 2)
                for g, (window, _) in enumerate(DILATED_GROUPS):
                    keep = st.seq if st.per_row else min(window, st.seq)
                    k_new, v_new = _kv_rows(qkv, st.batch, st.seq, keep, g, heads, head_dim)
                    kv_out[si][2 * g].append(k_new)
                    kv_out[si][2 * g + 1].append(v_new)
                tn = _pick(d, (1024, 512, 256, 128))
                if not st.per_row:
                    tma = st.row_tile(512)
                    parts = [_band_attention(qkv, st.batch, st.seq, g, heads, head_dim) for g in range(n_groups)]
                    a_ops = ([(o, pl.BlockSpec((tma, hd), lambda jj, ii: (ii, 0))) for o, _ in parts]
                             + [(l, pl.BlockSpec((tma, LANES), lambda jj, ii: (ii, 0))) for _, l in parts])
                    res = [(x, pl.BlockSpec((tma, tn), lambda jj, ii: (ii, jj))),
                           (st.mod_array(gate1), st.mod_spec(tma, tn, lambda jj, ii: ii, lambda jj, ii: jj))]
                    xs[si] = _matmul("attn_out", st.rows, a_ops,
                                     functools.partial(_a_merge, heads=heads, head_dim=head_dim),
                                     w_attn_o, j, res, _e_residual, tma, tn, F32,
                                     extra_vmem=2 * tma * (3 * hd + 3 * LANES + tn) * 4)
                else:
                    comb = _decode_attention(qkv, caches, j, st.batch, st.seq, heads, head_dim)
                    xs[si] = _matmul("attn_out", st.rows,
                                     [(comb, pl.BlockSpec((tm, hd), lambda jj, ii: (ii, 0)))], _a_cast,
                                     w_attn_o, j, res_ops(tn), _e_residual, tm, tn, F32,
                                     extra_vmem=2 * tm * (hd + tn) * 4)
            else:
                tn = _pick(2 * dr, (1024, 512, 256, 128))
                u = _matmul("rg_in", st.rows,
                            [(hn, pl.BlockSpec((tm, d), lambda jj, ii: (ii, 0)))], _a_cast,
                            w_rg_in, j, [], _e_store, tm, tn, F32, extra_vmem=2 * tm * d * 2)
                h0 = jnp.zeros((st.batch, dr), F32) if init_h[si] is None else init_h[si][j]
                cb0 = jnp.zeros((st.batch, CONV_W - 1, dr), F32) if init_conv[si] is None else init_conv[si][j]
                y, h_last, c_new = _rg_core(st, u, cb0, h0, conv_w[j], conv_b[j], w_gate_a[j], b_gate_a[j],
                                            w_gate_x[j], b_gate_x[j], lru_lambda[j],
                                            F32 if st.per_row else BF16)
                h_out[si].append(h_last)
                conv_out[si].append(c_new)
                tn = _pick(d, (1024, 512, 256, 128))
                xs[si] = _matmul("rg_out", st.rows,
                                 [(y, pl.BlockSpec((tm, dr), lambda jj, ii: (ii, 0)))], _a_cast,
                                 w_rg_out, j, res_ops(tn), _e_residual, tm, tn, F32,
                                 extra_vmem=2 * tm * (dr + tn) * 4)
        xs = _moe_layer(streams, xs, mods, g_norm2[layer], w_router, b_router,
                        w_exp1, b_exp1, w_exp2, b_exp2, layer)

    ys = [_final_norm(st, x, g_final).reshape(st.batch, st.seq, d) for st, x in zip(streams, xs)]
    outs = []
    for si in range(2):
        outs.append([jnp.stack(a, axis=0) for a in kv_out[si]]
                    + [jnp.stack(h_out[si], axis=0), jnp.stack(conv_out[si], axis=0)])
    return (ys[0], ys[1], *outs[0], *outs[1])
```

```python
import functools

import numpy as np
import jax
import jax.numpy as jnp
from jax import lax
from jax.experimental import pallas as pl
from jax.experimental.pallas import tpu as pltpu

F32 = jnp.float32
BF16 = jnp.bfloat16
I32 = jnp.int32

DILATED_GROUPS = ((128, 1), (512, 4), (2048, 16))
ROPE_THETA = 10000.0
PAST_LEN = 16384
CONV_W = 4
LRU_C = 8.0
TOP_K = 4
SWIGLU_LIMIT = 7.0
SWIGLU_ALPHA = 1.702
NORM_EPS = 1e-6

V7X_VMEM_BYTES = 64 * 1024 * 1024
VMEM_LIMIT_CAP = 56 * 1024 * 1024
LANES = 128
SUBLANES = 8

MOE_TILE = 1152
MOE_TJ = 256
MOE_ROW_SPLIT = 4
U32 = jnp.uint32


def _pack_halves(x):
    half = x.shape[1] // 2
    lo = lax.bitcast_convert_type(x[:, :half].astype(BF16).astype(F32), U32)
    hi = lax.bitcast_convert_type(x[:, half:].astype(BF16).astype(F32), U32)
    return (lo >> 16) | hi


def _unpack_halves(w):
    lo = lax.bitcast_convert_type(w << 16, F32).astype(BF16)
    hi = lax.bitcast_convert_type(w & U32(0xFFFF0000), F32).astype(BF16)
    return lo, hi


def _params(n_axes, vmem_bytes):
    limit = int(min(VMEM_LIMIT_CAP, max(vmem_bytes * 5 // 4 + (4 << 20), 16 << 20)))
    return pltpu.CompilerParams(dimension_semantics=("arbitrary",) * n_axes,
                                vmem_limit_bytes=limit)


def _pick(n, candidates):
    for c in candidates:
        if n % c == 0:
            return c
    return n


class _Stream:
    def __init__(self, batch, seq, per_row):
        self.batch, self.seq, self.per_row = batch, seq, per_row
        self.rows = batch * seq

    def mod_array(self, vec):
        if self.per_row:
            return jnp.repeat(vec, self.seq, axis=0)
        return vec[:, None, :]

    def mod_spec(self, tm, tn, row_of, col_of):
        if self.per_row:
            return pl.BlockSpec((tm, tn), lambda *g: (row_of(*g), col_of(*g)))
        per_batch = self.seq // tm
        return pl.BlockSpec((None, 1, tn), lambda *g: (row_of(*g) // per_batch, 0, col_of(*g)))

    def row_tile(self, cap):
        if self.per_row:
            return self.rows
        return _pick(self.seq, [c for c in (1024, 512, 256, 128, 64, 32, 16, 8) if c <= cap])


def _ada_body(c_ref, w_ref, b_ref, o_ref):
    c = c_ref[...]
    s = (c * jax.nn.sigmoid(c)).astype(BF16)
    o_ref[0] = jnp.dot(s, w_ref[0].astype(BF16), preferred_element_type=F32) + b_ref[0]


def _ada_mod(c_all, w_ada, b_ada):
    n_layers, d, n6 = w_ada.shape
    mc = c_all.shape[0]
    tn = _pick(n6, (1024, 512, 256, 128))
    vmem = 2 * (mc * d * 4 + d * tn * 4 + tn * 4 + mc * tn * 4) + d * tn * 2
    return pl.pallas_call(
        _ada_body,
        grid=(n_layers, n6 // tn),
        in_specs=[pl.BlockSpec((mc, d), lambda l, j: (0, 0)),
                  pl.BlockSpec((1, d, tn), lambda l, j: (l, 0, j)),
                  pl.BlockSpec((1, 1, tn), lambda l, j: (l, 0, j))],
        out_specs=pl.BlockSpec((1, mc, tn), lambda l, j: (l, 0, j)),
        out_shape=jax.ShapeDtypeStruct((n_layers, mc, n6), F32),
        compiler_params=_params(2, vmem),
        name="ada_mod",
    )(c_all, w_ada, b_ada.reshape(n_layers, 1, n6))


def _rms(x):
    return x * lax.rsqrt(jnp.mean(x * x, axis=-1, keepdims=True) + NORM_EPS)


def _norm_body(x_ref, g_ref, o_ref):
    o_ref[...] = (_rms(x_ref[...]) * g_ref[...]).astype(o_ref.dtype)


def _norm_mod_body(x_ref, g_ref, sc_ref, sh_ref, o_ref):
    hn = (_rms(x_ref[...]) * g_ref[...]) * (1.0 + sc_ref[...]) + sh_ref[...]
    o_ref[...] = hn.astype(o_ref.dtype)


def _final_norm(st, x, g):
    d = x.shape[1]
    tt = st.row_tile(512)
    return pl.pallas_call(
        _norm_body,
        grid=(st.rows // tt,),
        in_specs=[pl.BlockSpec((tt, d), lambda i: (i, 0)),
                  pl.BlockSpec((1, d), lambda i: (0, 0))],
        out_specs=pl.BlockSpec((tt, d), lambda i: (i, 0)),
        out_shape=jax.ShapeDtypeStruct((st.rows, d), F32),
        compiler_params=_params(1, 4 * tt * d * 4),
        name="final_norm",
    )(x, g.reshape(1, d))


def _norm_mod(st, x, g, scale, shift):
    d = x.shape[1]
    tt = st.row_tile(512)
    row_of, col_of = (lambda i: i), (lambda i: 0)
    return pl.pallas_call(
        _norm_mod_body,
        grid=(st.rows // tt,),
        in_specs=[pl.BlockSpec((tt, d), lambda i: (i, 0)),
                  pl.BlockSpec((1, d), lambda i: (0, 0)),
                  st.mod_spec(tt, d, row_of, col_of),
                  st.mod_spec(tt, d, row_of, col_of)],
        out_specs=pl.BlockSpec((tt, d), lambda i: (i, 0)),
        out_shape=jax.ShapeDtypeStruct((st.rows, d), BF16),
        compiler_params=_params(1, 2 * tt * d * (4 + 2) + 6 * tt * d * 4 * int(st.per_row)),
        name="norm_mod",
    )(x, g.reshape(1, d), st.mod_array(scale), st.mod_array(shift))


def _norm_route_body(x_ref, g_ref, sc_ref, sh_ref, wr_ref, br_ref, cin_ref,
                     hn_ref, idx_ref, gate_ref, rank_ref, cnt_ref, carry_ref):
    @pl.when(pl.program_id(0) == 0)
    def _():
        carry_ref[...] = cin_ref[...]

    hn = (_rms(x_ref[...]) * g_ref[...]) * (1.0 + sc_ref[...]) + sh_ref[...]
    packed = _pack_halves(hn)
    groups = packed.shape[1] // LANES
    for grp in range(groups):
        hn_ref[pl.ds(grp, packed.shape[0], stride=groups), :] = packed[:, grp * LANES:(grp + 1) * LANES]
    logits = lax.dot_general(wr_ref[...], hn, (((1,), (1,)), ((), ())),
                             precision=lax.Precision.HIGHEST,
                             preferred_element_type=F32) + br_ref[...]
    n_e, tt = logits.shape
    e_iota = lax.broadcasted_iota(I32, (n_e, tt), 0)
    vals, sels = [], []
    cur = logits
    for k in range(TOP_K):
        m = jnp.max(cur, axis=0, keepdims=True)
        idx = jnp.min(jnp.where(cur == m, e_iota, n_e), axis=0, keepdims=True)
        sel = e_iota == idx
        idx_ref[k:k + 1, :] = idx
        vals.append(m)
        sels.append(sel)
        cur = jnp.where(sel, -jnp.inf, cur)
    exps = [jnp.exp(v - vals[0]) for v in vals]
    den = exps[0]
    for e in exps[1:]:
        den = den + e
    upper = jnp.where(lax.broadcasted_iota(I32, (tt, tt), 0) <= lax.broadcasted_iota(I32, (tt, tt), 1),
                      1.0, 0.0).astype(BF16)
    base = carry_ref[...]
    for k in range(TOP_K):
        gate_ref[k:k + 1, :] = exps[k] / den
        onehot = jnp.where(sels[k], 1.0, 0.0)
        cum = jnp.dot(onehot.astype(BF16), upper, preferred_element_type=F32)
        rank = jnp.sum(onehot * (cum - 1.0 + base), axis=0, keepdims=True)
        rank_ref[k:k + 1, :] = rank.astype(I32)
        base = base + jnp.sum(onehot, axis=1, keepdims=True)
    carry_ref[...] = base
    cnt_ref[...] = base


def _norm_route(st, x, g, scale, shift, w_router_t, b_router, count_in):
    d = x.shape[1]
    n_e = w_router_t.shape[0]
    tt = st.row_tile(512)
    groups = d // 2 // LANES
    row_of, col_of = (lambda i: i), (lambda i: 0)
    tok_spec = pl.BlockSpec((TOP_K, tt), lambda i: (0, i))
    vmem = 2 * tt * d * (4 + 2) + 6 * tt * d * 4 * int(st.per_row) + tt * d * 8 + tt * tt * 8
    return pl.pallas_call(
        _norm_route_body,
        grid=(st.rows // tt,),
        in_specs=[pl.BlockSpec((tt, d), lambda i: (i, 0)),
                  pl.BlockSpec((1, d), lambda i: (0, 0)),
                  st.mod_spec(tt, d, row_of, col_of),
                  st.mod_spec(tt, d, row_of, col_of),
                  pl.BlockSpec((n_e, d), lambda i: (0, 0)),
                  pl.BlockSpec((n_e, 1), lambda i: (0, 0)),
                  pl.BlockSpec((n_e, 1), lambda i: (0, 0))],
        out_specs=[pl.BlockSpec((tt * groups, LANES), lambda i: (i, 0)), tok_spec, tok_spec, tok_spec,
                   pl.BlockSpec((n_e, 1), lambda i: (0, 0))],
        out_shape=[jax.ShapeDtypeStruct((st.rows * groups, LANES), U32),
                   jax.ShapeDtypeStruct((TOP_K, st.rows), I32),
                   jax.ShapeDtypeStruct((TOP_K, st.rows), F32),
                   jax.ShapeDtypeStruct((TOP_K, st.rows), I32),
                   jax.ShapeDtypeStruct((n_e, 1), F32)],
        scratch_shapes=[pltpu.VMEM((n_e, 1), F32)],
        compiler_params=_params(1, vmem),
        name="norm_route",
    )(x, g.reshape(1, d), st.mod_array(scale), st.mod_array(shift), w_router_t,
      b_router.reshape(n_e, 1), count_in)


def _mm_body(*refs, n_a, a_fn, n_e, e_fn):
    a_refs = refs[:n_a]
    w_ref = refs[n_a]
    e_refs = refs[n_a + 1:n_a + 1 + n_e]
    o_ref = refs[n_a + 1 + n_e]
    wbf_ref = refs[n_a + 2 + n_e]

    @pl.when(pl.program_id(1) == 0)
    def _():
        wbf_ref[...] = w_ref[...].astype(BF16)

    acc = jnp.dot(a_fn(*a_refs), wbf_ref[...], preferred_element_type=F32)
    e_fn(acc, o_ref, *e_refs)


def _matmul(name, rows, a_ops, a_fn, w_stack, layer, e_ops, e_fn, tm, tn, out_dtype, extra_vmem=0):
    _, k, n = w_stack.shape
    in_specs = ([s for _, s in a_ops]
                + [pl.BlockSpec((None, k, tn), lambda j, i: (layer, 0, j))]
                + [s for _, s in e_ops])
    vmem = 2 * k * tn * 4 + k * tn * 2 + 2 * tm * tn * 4 + extra_vmem
    return pl.pallas_call(
        functools.partial(_mm_body, n_a=len(a_ops), a_fn=a_fn, n_e=len(e_ops), e_fn=e_fn),
        grid=(n // tn, rows // tm),
        in_specs=in_specs,
        out_specs=pl.BlockSpec((tm, tn), lambda j, i: (i, j)),
        out_shape=jax.ShapeDtypeStruct((rows, n), out_dtype),
        scratch_shapes=[pltpu.VMEM((k, tn), BF16)],
        compiler_params=_params(2, vmem),
        name=name,
    )(*[a for a, _ in a_ops], w_stack, *[e for e, _ in e_ops])


def _a_cast(a_ref):
    return a_ref[...].astype(BF16)


def _e_store(acc, o_ref):
    o_ref[...] = acc.astype(o_ref.dtype)


def _e_residual(acc, o_ref, x_ref, gate_ref):
    o_ref[...] = x_ref[...] + gate_ref[...] * acc


def _e_rope(acc, o_ref, cos_ref, sin_ref, *, head_dim):
    is_v = pl.program_id(0) % 3 == 2

    @pl.when(is_v)
    def _():
        o_ref[...] = acc

    @pl.when(jnp.logical_not(is_v))
    def _():
        cos = cos_ref[...]
        sin = sin_ref[...]
        for h in range(acc.shape[1] // head_dim):
            t = acc[:, h * head_dim:(h + 1) * head_dim]
            o_ref[:, h * head_dim:(h + 1) * head_dim] = t * cos + pltpu.roll(t, head_dim // 2, 1) * sin


def _a_merge(o0_ref, o1_ref, o2_ref, l0_ref, l1_ref, l2_ref, *, heads, head_dim):
    lses = [l0_ref[...], l1_ref[...], l2_ref[...]]
    outs = [o0_ref, o1_ref, o2_ref]
    mx = jnp.maximum(jnp.maximum(lses[0], lses[1]), lses[2])
    es = [jnp.exp(l - mx) for l in lses]
    den = es[0] + es[1] + es[2]
    ws = [e / den for e in es]
    cols = []
    for h in range(heads):
        sl = slice(h * head_dim, (h + 1) * head_dim)
        c = ws[0][:, h:h + 1] * outs[0][:, sl]
        c = c + ws[1][:, h:h + 1] * outs[1][:, sl]
        c = c + ws[2][:, h:h + 1] * outs[2][:, sl]
        cols.append(c.astype(BF16))
    return jnp.concatenate(cols, axis=1)


def _attend(q, k_cat, v_cat, prev_bias, scale):
    nk = q.shape[0]
    s = lax.dot_general(q, k_cat, (((1,), (1,)), ((), ())), preferred_element_type=F32) * scale
    qi = lax.broadcasted_iota(I32, (nk, 2 * nk), 0)
    kj = lax.broadcasted_iota(I32, (nk, 2 * nk), 1)
    s = s + jnp.where(kj < nk, prev_bias, 0.0)
    s = jnp.where(jnp.logical_and(kj >= qi, kj <= qi + nk), s, -jnp.inf)
    m = jnp.max(s, axis=-1, keepdims=True)
    p = jnp.exp(s - m)
    l = jnp.sum(p, axis=-1, keepdims=True)
    o = jnp.dot((p / l).astype(BF16), v_cat, preferred_element_type=F32)
    return o, m + jnp.log(l)


def _band_dense_body(q_ref, kp_ref, kc_ref, vp_ref, vc_ref, o_ref, lse_ref, *, heads, head_dim, nk):
    scale = head_dim ** -0.5
    first_bias = jnp.where(pl.program_id(1) > 0, 0.0, -jnp.inf)
    lane = lax.broadcasted_iota(I32, (nk, LANES), 1)
    for s in range(q_ref.shape[0] // nk):
        rows = slice(s * nk, (s + 1) * nk)
        lse_all = jnp.zeros((nk, LANES), F32)
        for h in range(heads):
            sl = slice(h * head_dim, (h + 1) * head_dim)
            q = q_ref[rows, sl].astype(BF16)
            if s == 0:
                k_cat = jnp.concatenate([kp_ref[:, sl], kc_ref[rows, sl]], axis=0).astype(BF16)
                v_cat = jnp.concatenate([vp_ref[:, sl], vc_ref[rows, sl]], axis=0).astype(BF16)
                bias = first_bias
            else:
                k_cat = kc_ref[(s - 1) * nk:(s + 1) * nk, sl].astype(BF16)
                v_cat = vc_ref[(s - 1) * nk:(s + 1) * nk, sl].astype(BF16)
                bias = 0.0
            o, lse = _attend(q, k_cat, v_cat, bias, scale)
            o_ref[rows, sl] = o
            lse_all = jnp.where(lane == h, lse, lse_all)
        lse_ref[rows, :] = lse_all


def _band_span_body(q_ref, kp_ref, kc_ref, vp_ref, vc_ref, o_ref, lse_ref, *, dil, heads_per_step, head_dim):
    scale = head_dim ** -0.5
    nk = q_ref.shape[0] // dil
    head0 = pl.program_id(2) * heads_per_step
    prev_bias = jnp.where(pl.program_id(1) > 0, 0.0, -jnp.inf)
    lane = lax.broadcasted_iota(I32, (nk, LANES), 1)

    @pl.when(pl.program_id(2) == 0)
    def _():
        lse_ref[...] = jnp.zeros(lse_ref.shape, F32)

    for r in range(dil):
        rows = pl.ds(r, nk, stride=dil)
        lse_all = lse_ref[rows, :]
        for h in range(heads_per_step):
            sl = slice(h * head_dim, (h + 1) * head_dim)
            q = q_ref[rows, sl].astype(BF16)
            k_cat = jnp.concatenate([kp_ref[rows, sl], kc_ref[rows, sl]], axis=0).astype(BF16)
            v_cat = jnp.concatenate([vp_ref[rows, sl], vc_ref[rows, sl]], axis=0).astype(BF16)
            o, lse = _attend(q, k_cat, v_cat, prev_bias, scale)
            o_ref[rows, sl] = o
            lse_all = jnp.where(lane == head0 + h, lse, lse_all)
        lse_ref[rows, :] = lse_all


def _band_attention(qkv, batch, seq, g, heads, head_dim):
    window, dil = DILATED_GROUPS[g]
    nk = window // dil
    hd = heads * head_dim
    rows = batch * seq
    assert seq % window == 0, "prompt length must be a multiple of every window span"
    out_shape = [jax.ShapeDtypeStruct((rows, hd), F32), jax.ShapeDtypeStruct((rows, LANES), F32)]
    if dil == 1:
        tq = _pick(seq, (4 * nk, 2 * nk, nk))
        per, sub = seq // tq, tq // nk
        cur = lambda which: pl.BlockSpec((tq, hd), lambda b, n: (b * per + n, g * 3 + which))
        prev = lambda which: pl.BlockSpec(
            (nk, hd), lambda b, n: (b * per * sub + jnp.maximum(n * sub - 1, 0), g * 3 + which))
        return pl.pallas_call(
            functools.partial(_band_dense_body, heads=heads, head_dim=head_dim, nk=nk),
            grid=(batch, per),
            in_specs=[cur(0), prev(1), cur(1), prev(2), cur(2)],
            out_specs=[pl.BlockSpec((tq, hd), lambda b, n: (b * per + n, 0)),
                       pl.BlockSpec((tq, LANES), lambda b, n: (b * per + n, 0))],
            out_shape=out_shape,
            compiler_params=_params(2, 2 * (4 * tq + 2 * nk) * hd * 4 + (8 << 20)),
            name=f"band_attention_g{g}",
        )(qkv, qkv, qkv, qkv, qkv)
    assert head_dim == LANES
    hps = 1
    n_hs = heads // hps
    per = seq // window
    wide = hps * head_dim

    def spec(which, back):
        return pl.BlockSpec((window, wide),
                            lambda b, n, hs: (b * per + jnp.maximum(n - back, 0), (g * 3 + which) * n_hs + hs))

    return pl.pallas_call(
        functools.partial(_band_span_body, dil=dil, heads_per_step=hps, head_dim=head_dim),
        grid=(batch, per, n_hs),
        in_specs=[spec(0, 0), spec(1, 1), spec(1, 0), spec(2, 1), spec(2, 0)],
        out_specs=[pl.BlockSpec((window, wide), lambda b, n, hs: (b * per + n, hs)),
                   pl.BlockSpec((window, LANES), lambda b, n, hs: (b * per + n, 0))],
        out_shape=out_shape,
        compiler_params=_params(3, 2 * (6 * window * wide + window * LANES) * 4 + (8 << 20)),
        name=f"band_attention_g{g}",
    )(qkv, qkv, qkv, qkv, qkv)


def _kv_rows_body(k_ref, v_ref, ko_ref, vo_ref, *, heads):
    for h in range(heads):
        sl = slice(h * LANES, (h + 1) * LANES)
        ko_ref[:, h, :] = k_ref[:, sl]
        vo_ref[:, h, :] = v_ref[:, sl]


def _kv_rows(qkv, batch, seq, keep, g, heads, head_dim):
    assert head_dim == LANES
    hd = heads * head_dim
    tq = _pick(keep, (512, 256, 128, 64, 32, 16, 8))
    assert (seq - keep) % tq == 0
    per, first, per_seq = keep // tq, (seq - keep) // tq, seq // tq
    src = lambda which: pl.BlockSpec((tq, hd), lambda b, n: (b * per_seq + first + n, g * 3 + which))
    dst = pl.BlockSpec((None, tq, heads, head_dim), lambda b, n: (b, n, 0, 0))
    shape = jax.ShapeDtypeStruct((batch, keep, heads, head_dim), F32)
    return pl.pallas_call(
        functools.partial(_kv_rows_body, heads=heads),
        grid=(batch, per),
        in_specs=[src(1), src(2)],
        out_specs=[dst, dst],
        out_shape=[shape, shape],
        compiler_params=_params(2, 8 * tq * hd * 4),
        name="kv_rows",
    )(qkv, qkv)


def _decode_body(qkv_ref, *refs, heads, head_dim, seq):
    cache_refs = refs[:-1]
    o_ref = refs[-1]
    hd = heads * head_dim
    scale = head_dim ** -0.5
    nt = (((1,), (1,)), ((), ()))
    diag = (lax.broadcasted_iota(I32, (heads, hd), 0)
            == lax.broadcasted_iota(I32, (heads, hd), 1) // head_dim)
    for t in range(seq):
        outs, lses = [], []
        for g, (window, dil) in enumerate(DILATED_GROUPS):
            nk = window // dil
            base = g * 3 * hd
            q = jnp.broadcast_to(qkv_ref[t:t + 1, base:base + hd], (heads, hd))
            qbd = jnp.where(diag, q, 0.0).astype(BF16)
            k_new = qkv_ref[:, base + hd:base + 2 * hd].astype(BF16)
            v_new = qkv_ref[:, base + 2 * hd:base + 3 * hd].astype(BF16)
            res, first = t % dil, t // dil
            k_old = jnp.concatenate([cache_refs[2 * g][:, res * heads + h, :] for h in range(heads)],
                                    axis=1).astype(BF16)
            v_old = jnp.concatenate([cache_refs[2 * g + 1][:, res * heads + h, :] for h in range(heads)],
                                    axis=1).astype(BF16)
            s_old = lax.dot_general(qbd, k_old, nt, preferred_element_type=F32) * scale
            s_new = lax.dot_general(qbd, k_new, nt, preferred_element_type=F32) * scale
            mi = lax.broadcasted_iota(I32, (heads, nk), 1)
            ti = lax.broadcasted_iota(I32, (heads, seq), 1)
            s_old = jnp.where(mi >= first, s_old, -jnp.inf)
            s_new = jnp.where(jnp.logical_and(ti <= t, ((t - ti) & (dil - 1)) == 0), s_new, -jnp.inf)
            m = jnp.maximum(jnp.max(s_old, axis=-1, keepdims=True), jnp.max(s_new, axis=-1, keepdims=True))
            p_old = jnp.exp(s_old - m)
            p_new = jnp.exp(s_new - m)
            l = jnp.sum(p_old, axis=-1, keepdims=True) + jnp.sum(p_new, axis=-1, keepdims=True)
            o = (jnp.dot((p_old / l).astype(BF16), v_old, preferred_element_type=F32)
                 + jnp.dot((p_new / l).astype(BF16), v_new, preferred_element_type=F32))
            outs.append(o)
            lses.append(m + jnp.log(l))
        mx = jnp.maximum(jnp.maximum(lses[0], lses[1]), lses[2])
        es = [jnp.exp(l - mx) for l in lses]
        den = es[0] + es[1] + es[2]
        comb = (es[0] / den) * outs[0] + (es[1] / den) * outs[1] + (es[2] / den) * outs[2]
        o_ref[t:t + 1, :] = jnp.sum(jnp.where(diag, comb, 0.0), axis=0, keepdims=True)


def _decode_attention(qkv, caches, layer, batch, seq, heads, head_dim):
    hd = heads * head_dim
    ops, specs = [], []
    vmem = seq * qkv.shape[1] * 4
    for g, (window, dil) in enumerate(DILATED_GROUPS):
        assert dil & (dil - 1) == 0 and seq <= window
        nk = window // dil
        for c in caches[2 * g:2 * g + 2]:
            assert c.shape[2] == window, "the cache must hold exactly one window of past positions"
            n_res = min(dil, seq)
            ops.append(c.reshape(c.shape[0], c.shape[1], nk, dil * heads, head_dim))
            specs.append(pl.BlockSpec((None, None, nk, n_res * heads, head_dim), lambda b: (layer, b, 0, 0, 0)))
            vmem += nk * n_res * hd * 4
    return pl.pallas_call(
        functools.partial(_decode_body, heads=heads, head_dim=head_dim, seq=seq),
        grid=(batch,),
        in_specs=[pl.BlockSpec((seq, qkv.shape[1]), lambda b: (b, 0))] + specs,
        out_specs=pl.BlockSpec((seq, hd), lambda b: (b, 0)),
        out_shape=jax.ShapeDtypeStruct((batch * seq, hd), F32),
        compiler_params=_params(1, 2 * vmem),
        name="decode_attention",
    )(qkv, *ops)


def _softplus(z):
    return jnp.maximum(z, 0.0) + jnp.log1p(jnp.exp(-jnp.abs(z)))


def _rg_body(gate_ref, xr_ref, cbuf_ref, cw_ref, cb_ref, wga_ref, bga_ref, wgx_ref, bgx_ref,
             lam_ref, h0_ref, y_ref, hlast_ref, cnew_ref, xs_ref, a_ref, b_ref, h_ref,
             *, tt, n_blocks):
    ti = pl.program_id(1)
    halo = CONV_W - 1
    lo = SUBLANES - halo

    @pl.when(ti == 0)
    def _():
        xs_ref[lo:SUBLANES, :] = cbuf_ref[...]
        h_ref[...] = h0_ref[...]

    @pl.when(ti > 0)
    def _():
        xs_ref[lo:SUBLANES, :] = xs_ref[tt + lo:tt + SUBLANES, :]

    xs_ref[SUBLANES:SUBLANES + tt, :] = xr_ref[...]
    conv = cw_ref[0:1, :] * xs_ref[lo:lo + tt, :]
    for tap in range(1, CONV_W):
        conv = conv + cw_ref[tap:tap + 1, :] * xs_ref[lo + tap:lo + tap + tt, :]
    xc = cb_ref[...] + conv

    @pl.when(ti == pl.num_programs(1) - 1)
    def _():
        cnew_ref[...] = xs_ref[tt + lo:tt + SUBLANES, :]

    xcb = xc.astype(BF16)
    bw = xc.shape[1] // n_blocks
    for n in range(n_blocks):
        sl = slice(n * bw, (n + 1) * bw)
        ga = jnp.dot(xcb[:, sl], wga_ref[n].astype(BF16), preferred_element_type=F32) + bga_ref[:, sl]
        gx = jnp.dot(xcb[:, sl], wgx_ref[n].astype(BF16), preferred_element_type=F32) + bgx_ref[:, sl]
        r = jax.nn.sigmoid(ga)
        i = jax.nn.sigmoid(gx)
        log_a = -LRU_C * r * _softplus(-lam_ref[:, sl])
        a = jnp.exp(log_a)
        a_ref[:, sl] = a
        b_ref[:, sl] = jnp.sqrt(-jnp.tanh(log_a) * (a * a + 1.0)) * (i * xc[:, sl])

    sub = lax.broadcasted_iota(I32, (SUBLANES, a_ref.shape[1]), 0)

    def group(gi, h):
        rows = pl.ds(pl.multiple_of(gi * SUBLANES, SUBLANES), SUBLANES)
        ca = a_ref[rows, :]
        cb = b_ref[rows, :]
        for shift in (1, 2, 4):
            pa = pltpu.roll(ca, shift, 0)
            pb = pltpu.roll(cb, shift, 0)
            cb = jnp.where(sub >= shift, cb + ca * pb, cb)
            ca = jnp.where(sub >= shift, ca * pa, ca)
        hs = ca * h + cb
        b_ref[rows, :] = hs
        return hs[SUBLANES - 1:SUBLANES, :]

    h = lax.fori_loop(0, tt // SUBLANES, group, h_ref[...])
    h_ref[...] = h

    @pl.when(ti == pl.num_programs(1) - 1)
    def _():
        hlast_ref[...] = h

    y_ref[...] = (b_ref[...] * jax.nn.gelu(gate_ref[...])).astype(y_ref.dtype)


def _rg_core(st, u, conv_buf, h0, conv_w, conv_b, w_ga, b_ga, w_gx, b_gx, lam, out_dtype):
    dr = u.shape[1] // 2
    n_blocks, bw, _ = w_ga.shape
    batch, seq = st.batch, st.seq
    tt = _pick(seq, (256, 128, 64, 32, 16, 8))
    per = seq // tt
    halo = CONV_W - 1
    vec = lambda: pl.BlockSpec((1, dr), lambda b, t: (0, 0))
    gate_w = lambda: pl.BlockSpec((n_blocks, bw, bw), lambda b, t: (0, 0, 0))
    vmem = 2 * (2 * tt * dr * 4 + 2 * n_blocks * bw * bw * 4 + tt * dr * 4) + 3 * (tt + 8) * dr * 4 + 6 * tt * dr * 4
    y, h_last, c_new = pl.pallas_call(
        functools.partial(_rg_body, tt=tt, n_blocks=n_blocks),
        grid=(batch, per),
        in_specs=[pl.BlockSpec((tt, dr), lambda b, t: (b * per + t, 0)),
                  pl.BlockSpec((tt, dr), lambda b, t: (b * per + t, 1)),
                  pl.BlockSpec((None, halo, dr), lambda b, t: (b, 0, 0)),
                  pl.BlockSpec((CONV_W, dr), lambda b, t: (0, 0)),
                  vec(), gate_w(), vec(), gate_w(), vec(), vec(),
                  pl.BlockSpec((None, 1, dr), lambda b, t: (b, 0, 0))],
        out_specs=[pl.BlockSpec((tt, dr), lambda b, t: (b * per + t, 0)),
                   pl.BlockSpec((None, 1, dr), lambda b, t: (b, 0, 0)),
                   pl.BlockSpec((None, halo, dr), lambda b, t: (b, 0, 0))],
        out_shape=[jax.ShapeDtypeStruct((st.rows, dr), out_dtype),
                   jax.ShapeDtypeStruct((batch, 1, dr), F32),
                   jax.ShapeDtypeStruct((batch, halo, dr), F32)],
        scratch_shapes=[pltpu.VMEM((tt + SUBLANES, dr), F32), pltpu.VMEM((tt, dr), F32),
                        pltpu.VMEM((tt, dr), F32), pltpu.VMEM((1, dr), F32)],
        compiler_params=_params(2, vmem),
        name="rg_core",
    )(u, u, conv_buf, conv_w, conv_b.reshape(1, dr), w_ga, b_ga.reshape(1, dr), w_gx,
      b_gx.reshape(1, dr), lam.reshape(1, dr), h0.reshape(batch, 1, dr))
    return y, h_last.reshape(batch, dr), c_new


def _moe_body(te_ref, tv_ref, tb_ref, nr_ref, tok_ref, dst_ref,
              hn_hbm, w1g_ref, w1l_ref, b1g_ref, b1l_ref, w2_ref, b2_ref, y_hbm,
              xbuf, acc, ystage, wg_s, wl_s, w2_s, gsem, ssem, *, n_tiles, nj):
    i = pl.program_id(0)
    j = pl.program_id(1)
    n_real = nr_ref[0]
    slot = i % 2
    other = 1 - slot
    last = n_real - 1
    tile, d = acc.shape
    gx = xbuf.shape[1] // tile
    gy = d // LANES
    per_gather = tile // nj
    per_scatter = -(-tile // (nj - 1))

    def start_gather(t, s, r):
        tok = pl.multiple_of(tok_ref[tb_ref[t] + r], gx)
        pltpu.make_async_copy(hn_hbm.at[pl.ds(tok, gx)], xbuf.at[s, pl.ds(pl.multiple_of(r * gx, gx), gx)],
                              gsem.at[s]).start()

    def wait_gathers(s):
        pltpu.make_async_copy(hn_hbm.at[pl.ds(0, tile * gx)], xbuf.at[s], gsem.at[s]).wait()

    def start_scatter(t, r):
        dst = pl.multiple_of(dst_ref[tb_ref[t] + r], gy)
        pltpu.make_async_copy(ystage.at[pl.ds(pl.multiple_of(r * gy, gy), gy)], y_hbm.at[pl.ds(dst, gy)],
                              ssem).start(priority=1)

    def wait_scatters(t, enabled):
        n = tv_ref[t]
        rows = 1
        while rows <= tile:
            @pl.when(jnp.logical_and(enabled, (n & rows) != 0))
            def _():
                pltpu.make_async_copy(ystage.at[pl.ds(0, rows * gy)], y_hbm.at[pl.ds(0, rows * gy)], ssem).wait()
            rows *= 2

    def column_step(final):
        nxt = jnp.minimum(i + 1, last)
        prev = jnp.maximum(i - 1, 0)
        n_scatter = 0 if final else per_scatter
        n_batches = 2 * MOE_ROW_SPLIT

        def issue_batch(b):
            for r in range(per_gather * b // n_batches, per_gather * (b + 1) // n_batches):
                start_gather(nxt, other, j * per_gather + r)
            for r in range(n_scatter * b // n_batches, n_scatter * (b + 1) // n_batches):
                rr = j * per_scatter + r

                @pl.when(jnp.logical_and(i >= 1, rr < tv_ref[prev]))
                def _():
                    start_scatter(prev, rr)

        wg_s[...] = w1g_ref[...].astype(BF16)
        wl_s[...] = w1l_ref[...].astype(BF16)
        w2_s[...] = w2_ref[...].astype(BF16)
        part = tile // MOE_ROW_SPLIT
        for p in range(MOE_ROW_SPLIT):
            rows = slice(p * part, (p + 1) * part)
            issue_batch(2 * p)
            halves = [_unpack_halves(xbuf[slot, pl.ds(p * part * gx + grp, part, stride=gx), :])
                      for grp in range(gx)]
            x = jnp.concatenate([lo for lo, _ in halves] + [hi for _, hi in halves], axis=1)
            glu = jnp.dot(x, wg_s[...], preferred_element_type=F32) + b1g_ref[...]
            lin = jnp.dot(x, wl_s[...], preferred_element_type=F32) + b1l_ref[...]
            glu = jnp.minimum(glu, SWIGLU_LIMIT)
            lin = jnp.clip(lin, -SWIGLU_LIMIT, SWIGLU_LIMIT)
            act = glu * jax.nn.sigmoid(SWIGLU_ALPHA * glu) * (lin + 1.0)
            issue_batch(2 * p + 1)
            res = acc[rows, :] + jnp.dot(act.astype(BF16), w2_s[...], preferred_element_type=F32)
            if final:
                for grp in range(gy):
                    ystage[pl.ds(p * part * gy + grp, part, stride=gy), :] = res[:, grp * LANES:(grp + 1) * LANES]
            else:
                acc[rows, :] = res

    @pl.when(i < n_real)
    def _():
        @pl.when(j == 0)
        def _():
            @pl.when(i == 0)
            def _():
                def first_rows(r, carry):
                    start_gather(0, 0, r)
                    return carry
                lax.fori_loop(0, tile, first_rows, 0)

            wait_gathers(slot)
            acc[...] = jnp.broadcast_to(b2_ref[...], acc.shape)

        @pl.when(j < nj - 1)
        def _():
            column_step(False)

        @pl.when(j == nj - 1)
        def _():
            wait_scatters(jnp.maximum(i - 1, 0), i >= 1)
            column_step(True)

    @pl.when(jnp.logical_and(i == n_tiles - 1, j == nj - 1))
    def _():
        def last_rows(r, carry):
            start_scatter(last, r)
            return carry
        lax.fori_loop(0, tv_ref[last], last_rows, 0)
        wait_gathers(n_real % 2)
        wait_scatters(last, True)


def _moe_experts(hn_packed, tok_sorted, dst_sorted, tile_expert, tile_valid, tile_base, n_real, w1, b1, w2, b2, layer):
    _, n_e, d, two_de = w1.shape
    half = d // 2
    gx, gy = half // LANES, d // LANES
    n_tok = hn_packed.shape[0] // gx
    de = two_de // 2
    tj = _pick(de, (MOE_TJ, 128))
    nj = de // tj
    n_tiles = tile_expert.shape[0]
    assert nj >= 2 and MOE_TILE % nj == 0 and MOE_TILE % MOE_ROW_SPLIT == 0 and n_tok >= MOE_TILE

    def tile(i, nr):
        return jnp.minimum(i, nr[0] - 1)

    def col(i, j, nr):
        return jnp.where(i < nr[0], j, nj - 1)

    grid_spec = pltpu.PrefetchScalarGridSpec(
        num_scalar_prefetch=6,
        grid=(n_tiles, nj),
        in_specs=[
            pl.BlockSpec(memory_space=pl.ANY),
            pl.BlockSpec((None, None, d, tj), lambda i, j, te, tv, tb, nr, tok, dst: (layer, te[tile(i, nr)], 0, col(i, j, nr))),
            pl.BlockSpec((None, None, d, tj), lambda i, j, te, tv, tb, nr, tok, dst: (layer, te[tile(i, nr)], 0, nj + col(i, j, nr))),
            pl.BlockSpec((None, None, 1, tj), lambda i, j, te, tv, tb, nr, tok, dst: (layer, te[tile(i, nr)], 0, col(i, j, nr))),
            pl.BlockSpec((None, None, 1, tj), lambda i, j, te, tv, tb, nr, tok, dst: (layer, te[tile(i, nr)], 0, nj + col(i, j, nr))),
            pl.BlockSpec((None, None, tj, d), lambda i, j, te, tv, tb, nr, tok, dst: (layer, te[tile(i, nr)], col(i, j, nr), 0)),
            pl.BlockSpec((None, None, 1, d), lambda i, j, te, tv, tb, nr, tok, dst: (layer, te[tile(i, nr)], 0, 0)),
        ],
        out_specs=pl.BlockSpec(memory_space=pl.ANY),
        scratch_shapes=[pltpu.VMEM((2, MOE_TILE * gx, LANES), U32), pltpu.VMEM((MOE_TILE, d), F32),
                        pltpu.VMEM((MOE_TILE * gy, LANES), F32),
                        pltpu.VMEM((d, tj), BF16), pltpu.VMEM((d, tj), BF16), pltpu.VMEM((tj, d), BF16),
                        pltpu.SemaphoreType.DMA((2,)), pltpu.SemaphoreType.DMA(())],
    )
    part = MOE_TILE // MOE_ROW_SPLIT
    vmem = (2 * MOE_TILE * half * 4 + 2 * MOE_TILE * d * 4 + 2 * 3 * d * tj * 4 + 3 * d * tj * 2
            + part * (d * 2 + 4 * tj * 4 + d * 4))
    return pl.pallas_call(
        functools.partial(_moe_body, n_tiles=n_tiles, nj=nj),
        grid_spec=grid_spec,
        out_shape=jax.ShapeDtypeStruct((TOP_K * n_tok * gy, LANES), F32),
        compiler_params=pltpu.CompilerParams(dimension_semantics=("arbitrary", "arbitrary"),
                                             vmem_limit_bytes=int(min(VMEM_LIMIT_CAP, vmem + (6 << 20)))),
        name="moe_experts",
    )(tile_expert, tile_valid, tile_base, n_real, tok_sorted, dst_sorted, hn_packed, w1, w1,
      b1.reshape(b1.shape[0], n_e, 1, two_de), b1.reshape(b1.shape[0], n_e, 1, two_de),
      w2, b2.reshape(b2.shape[0], n_e, 1, d))


def _combine_body(x_ref, y_ref, gt_ref, g2_ref, o_ref):
    tt, d = x_ref.shape
    gy = d // LANES
    for grp in range(gy):
        sl = slice(grp * LANES, (grp + 1) * LANES)
        acc = y_ref[0, pl.ds(grp, tt, stride=gy), :] * gt_ref[:, 0:1]
        for k in range(1, TOP_K):
            acc = acc + y_ref[k, pl.ds(grp, tt, stride=gy), :] * gt_ref[:, k:k + 1]
        o_ref[:, sl] = x_ref[:, sl] + g2_ref[:, sl] * acc


def _moe_combine(st, x, y_all, first_row, gates_t, gate2):
    d = x.shape[1]
    gy = d // LANES
    tt = st.row_tile(256)
    assert first_row % tt == 0
    first = first_row // tt
    row_of, col_of = (lambda i: i), (lambda i: 0)
    return pl.pallas_call(
        _combine_body,
        grid=(st.rows // tt,),
        in_specs=[pl.BlockSpec((tt, d), lambda i: (i, 0)),
                  pl.BlockSpec((TOP_K, tt * gy, LANES), lambda i: (0, first + i, 0)),
                  pl.BlockSpec((tt, TOP_K), lambda i: (i, 0)),
                  st.mod_spec(tt, d, row_of, col_of)],
        out_specs=pl.BlockSpec((tt, d), lambda i: (i, 0)),
        out_shape=jax.ShapeDtypeStruct((st.rows, d), F32),
        compiler_params=_params(1, 2 * tt * d * 4 * (3 + TOP_K)),
        name="moe_combine",
    )(x, y_all, gates_t, st.mod_array(gate2))


def _moe_layer(streams, xs, mods, g2, w_router, b_router, w1, b1, w2, b2, layer):
    n_e = w_router.shape[-1]
    w_router_t = w_router[layer].T
    counts = jnp.zeros((n_e, 1), F32)
    hns, idxs, gates, ranks = [], [], [], []
    for st, x, (shift2, scale2, _) in zip(streams, xs, mods):
        hn, idx, gate, rank, counts = _norm_route(st, x, g2, scale2, shift2, w_router_t,
                                                  b_router[layer], counts)
        hns.append(hn), idxs.append(idx), gates.append(gate), ranks.append(rank)
    hn_all = jnp.concatenate(hns, axis=0)
    idx_all = jnp.concatenate(idxs, axis=1)
    rank_all = jnp.concatenate(ranks, axis=1)
    d = xs[0].shape[1]
    gx, gy = d // 2 // LANES, d // LANES
    n_tok = hn_all.shape[0] // gx

    n_assign = n_tok * TOP_K
    n_tiles = n_e + n_assign // MOE_TILE
    cnt = counts[:, 0].astype(I32)
    tiles_e = (cnt + MOE_TILE - 1) // MOE_TILE
    tile_end = jnp.cumsum(tiles_e)
    tile_start = tile_end - tiles_e
    n_real = tile_end[-1:]
    expert_start = jnp.cumsum(cnt) - cnt
    tile_ids = jnp.arange(n_tiles, dtype=I32)
    tile_expert = jnp.minimum(jnp.searchsorted(tile_end, tile_ids, side='right'), n_e - 1).astype(I32)
    tile_off = (tile_ids - tile_start[tile_expert]) * MOE_TILE
    tile_valid = jnp.clip(cnt[tile_expert] - tile_off, 0, MOE_TILE).astype(I32)
    tile_base = jnp.clip(expert_start[tile_expert] + tile_off, 0, n_assign - 1).astype(I32)
    onehot = idx_all[:, :, None] == jnp.arange(n_e, dtype=I32)[None, None, :]
    pos = jnp.sum(jnp.where(onehot, expert_start[None, None, :], 0), axis=-1) + rank_all
    asg_sorted = jnp.concatenate([jnp.argsort(pos.reshape(-1)).astype(I32), jnp.zeros((MOE_TILE,), I32)])
    tok_sorted = (asg_sorted % n_tok) * gx
    dst_sorted = asg_sorted * gy

    y_all = _moe_experts(hn_all, tok_sorted, dst_sorted, tile_expert, tile_valid, tile_base, n_real,
                         w1, b1, w2, b2, layer).reshape(TOP_K, n_tok * gy, LANES)

    new_xs, off = [], 0
    for st, x, gate, (_, _, gate2) in zip(streams, xs, gates, mods):
        new_xs.append(_moe_combine(st, x, y_all, off, gate.T, gate2))
        off += st.rows
    return new_xs


def _rope_tables(positions, head_dim):
    inv_freq = ROPE_THETA ** (-jnp.arange(0, head_dim, 2, dtype=F32) / head_dim)
    ang = positions.astype(F32)[:, None] * inv_freq[None, :]
    cos, sin = jnp.cos(ang), jnp.sin(ang)
    return jnp.concatenate([cos, cos], axis=-1), jnp.concatenate([-sin, sin], axis=-1)


def kernel(x_prompt, x_sample, c_prompt, c_sample, cache_k_w128, cache_v_w128, cache_k_w512, cache_v_w512, cache_k_w2048, cache_v_w2048, state_rglru_h, state_conv, w_ada, b_ada, g_norm1, g_norm2, w_qkv, w_attn_o, w_rg_in, conv_w, conv_b, w_gate_a, b_gate_a, w_gate_x, b_gate_x, lru_lambda, w_rg_out, w_router, b_router, w_exp1, b_exp1, w_exp2, b_exp2, g_final):
    caches = (cache_k_w128, cache_v_w128, cache_k_w512, cache_v_w512, cache_k_w2048, cache_v_w2048)
    bp, tp, d = x_prompt.shape
    bs, ts, _ = x_sample.shape
    depth = w_ada.shape[0]
    heads, head_dim = cache_k_w128.shape[3], cache_k_w128.shape[4]
    hd = heads * head_dim
    n_groups = len(DILATED_GROUPS)
    dr = w_rg_out.shape[1]
    assert w_qkv.shape[2] == n_groups * 3 * hd

    prompt = _Stream(bp, tp, per_row=False)
    sample = _Stream(bs, ts, per_row=True)
    streams = (prompt, sample)

    n_c = bp + bs
    n_c_pad = -(-n_c // SUBLANES) * SUBLANES
    c_all = jnp.concatenate([c_prompt, c_sample, jnp.zeros((n_c_pad - n_c, d), F32)], axis=0)
    mod = _ada_mod(c_all, w_ada, b_ada).reshape(depth, n_c_pad, 6, d)
    row_range = ((0, bp), (bp, bp + bs))

    cos_p, sin_p = _rope_tables(jnp.arange(tp, dtype=I32), head_dim)
    cos_s, sin_s = _rope_tables(PAST_LEN + jnp.arange(ts, dtype=I32), head_dim)
    rope = ((cos_p, sin_p), (jnp.tile(cos_s, (bs, 1)), jnp.tile(sin_s, (bs, 1))))

    xs = [x_prompt.reshape(bp * tp, d), x_sample.reshape(bs * ts, d)]
    kv_out = [[[] for _ in range(2 * n_groups)] for _ in streams]
    h_out = [[] for _ in streams]
    conv_out = [[] for _ in streams]
    init_h = (None, state_rglru_h)
    init_conv = (None, state_conv)

    for layer in range(depth):
        j = layer // 2
        mods = []
        for si, st in enumerate(streams):
            lo, hi = row_range[si]
            shift1, scale1, gate1, shift2, scale2, gate2 = [mod[layer, lo:hi, n] for n in range(6)]
            mods.append((shift2, scale2, gate2))
            x = xs[si]
            hn = _norm_mod(st, x, g_norm1[layer], scale1, shift1)
            tm = st.row_tile(1024)
            res_ops = lambda tn: [(x, pl.BlockSpec((tm, tn), lambda jj, ii: (ii, jj))),
                                  (st.mod_array(gate1), st.mod_spec(tm, tn, lambda jj, ii: ii, lambda jj, ii: jj))]
            if layer % 2 == 0:
                cos, sin = rope[si]
                per = max(st.seq // tm, 1)
                tab = (pl.BlockSpec((tm, head_dim), lambda jj, ii: (ii % per, 0)) if not st.per_row
                       else pl.BlockSpec((tm, head_dim), lambda jj, ii: (0, 0)))
                qkv = _matmul("qkv_rope", st.rows,
                              [(hn, pl.BlockSpec((tm, d), lambda jj, ii: (ii, 0)))], _a_cast,
                              w_qkv, j, [(cos, tab), (sin, tab)],
                              functools.partial(_e_rope, head_dim=head_dim),
                              tm, hd, F32, extra_vmem=2 * tm * d * 2)
                for g, (window, _) in enumerate(DILATED_GROUPS):
                    keep = st.seq if st.per_row else min(window, st.seq)
                    k_new, v_new = _kv_rows(qkv, st.batch, st.seq, keep, g, heads, head_dim)
                    kv_out[si][2 * g].append(k_new)
                    kv_out[si][2 * g + 1].append(v_new)
                tn = _pick(d, (1024, 512, 256, 128))
                if not st.per_row:
                    tma = st.row_tile(512)
                    parts = [_band_attention(qkv, st.batch, st.seq, g, heads, head_dim) for g in range(n_groups)]
                    a_ops = ([(o, pl.BlockSpec((tma, hd), lambda jj, ii: (ii, 0))) for o, _ in parts]
                             + [(l, pl.BlockSpec((tma, LANES), lambda jj, ii: (ii, 0))) for _, l in parts])
                    res = [(x, pl.BlockSpec((tma, tn), lambda jj, ii: (ii, jj))),
                           (st.mod_array(gate1), st.mod_spec(tma, tn, lambda jj, ii: ii, lambda jj, ii: jj))]
                    xs[si] = _matmul("attn_out", st.rows, a_ops,
                                     functools.partial(_a_merge, heads=heads, head_dim=head_dim),
                                     w_attn_o, j, res, _e_residual, tma, tn, F32,
                                     extra_vmem=2 * tma * (3 * hd + 3 * LANES + tn) * 4)
                else:
                    comb = _decode_attention(qkv, caches, j, st.batch, st.seq, heads, head_dim)
                    xs[si] = _matmul("attn_out", st.rows,
                                     [(comb, pl.BlockSpec((tm, hd), lambda jj, ii: (ii, 0)))], _a_cast,
                                     w_attn_o, j, res_ops(tn), _e_residual, tm, tn, F32,
                                     extra_vmem=2 * tm * (hd + tn) * 4)
            else:
                tn = _pick(2 * dr, (1024, 512, 256, 128))
                u = _matmul("rg_in", st.rows,
                            [(hn, pl.BlockSpec((tm, d), lambda jj, ii: (ii, 0)))], _a_cast,
                            w_rg_in, j, [], _e_store, tm, tn, F32, extra_vmem=2 * tm * d * 2)
                h0 = jnp.zeros((st.batch, dr), F32) if init_h[si] is None else init_h[si][j]
                cb0 = jnp.zeros((st.batch, CONV_W - 1, dr), F32) if init_conv[si] is None else init_conv[si][j]
                y, h_last, c_new = _rg_core(st, u, cb0, h0, conv_w[j], conv_b[j], w_gate_a[j], b_gate_a[j],
                                            w_gate_x[j], b_gate_x[j], lru_lambda[j],
                                            F32 if st.per_row else BF16)
                h_out[si].append(h_last)
                conv_out[si].append(c_new)
                tn = _pick(d, (1024, 512, 256, 128))
                xs[si] = _matmul("rg_out", st.rows,
                                 [(y, pl.BlockSpec((tm, dr), lambda jj, ii: (ii, 0)))], _a_cast,
                                 w_rg_out, j, res_ops(tn), _e_residual, tm, tn, F32,
                                 extra_vmem=2 * tm * (dr + tn) * 4)
        xs = _moe_layer(streams, xs, mods, g_norm2[layer], w_router, b_router,
                        w_exp1, b_exp1, w_exp2, b_exp2, layer)

    ys = [_final_norm(st, x, g_final).reshape(st.batch, st.seq, d) for st, x in zip(streams, xs)]
    outs = []
    for si in range(2):
        outs.append([jnp.stack(a, axis=0) for a in kv_out[si]]
                    + [jnp.stack(h_out[si], axis=0), jnp.stack(conv_out[si], axis=0)])
    return (ys[0], ys[1], *outs[0], *outs[1])
```
